```python
import math
import jax, jax.numpy as jnp
from jax import lax
import numpy as np

D_MODEL = 1024
BATCH = 16
SEQ = 2048
DEPTH = 2

HEAD_DIM = 64
FOX_HEADS = 4
DIFF_HEADS = 4
DIL_HEADS = 8
FOX_W = FOX_HEADS * HEAD_DIM
DIFF_W = DIFF_HEADS * HEAD_DIM
DIL_W = DIL_HEADS * HEAD_DIM
MIX_W = FOX_W + DIFF_W + DIL_W
DIFF_QK_DIM = HEAD_DIM // 2
IN_COLS = 3 * MIX_W + FOX_HEADS
Q_BLOCK = 128
DIL_CONFIGS = ((128, 1), (512, 4), (2048, 16))
MEM_TOKENS = 256
MEM_HEADS = 4
MEM_HEAD_DIM = D_MODEL // MEM_HEADS
D_FF = 2816
CONV_WIDTH = 3
ALPHA = (2 * DEPTH) ** 0.25
BETA = (8 * DEPTH) ** -0.25
LN_EPS = 1e-5
NEG_INF = -1e30
FORGET_BIAS_INIT = 3.0

kernel_name = 'hybrid_fox_diff_dilated_postnorm_decoder'


def layer_norm(x, g, b):
    xf = x.astype(jnp.float32)
    mu = jnp.mean(xf, axis=-1, keepdims=True)
    var = jnp.mean(jnp.square(xf - mu), axis=-1, keepdims=True)
    y = (xf - mu) * lax.rsqrt(var + LN_EPS) * g.astype(jnp.float32) + b.astype(jnp.float32)
    return y.astype(x.dtype)


def rms_norm(x, g):
    xf = x.astype(jnp.float32)
    return xf * lax.rsqrt(jnp.mean(jnp.square(xf), axis=-1, keepdims=True) + LN_EPS) * g.astype(jnp.float32)


def alibi_slopes():
    n = DIFF_HEADS + DIL_HEADS
    slopes = 2.0 ** (-8.0 * np.arange(1, n + 1) / n)
    stride = n // DIFF_HEADS
    diff_sel = np.arange(0, n, stride)[:DIFF_HEADS]
    dil_sel = np.setdiff1d(np.arange(n), diff_sel)
    return (jnp.asarray(slopes[diff_sel], dtype=jnp.float32),
            jnp.asarray(slopes[dil_sel], dtype=jnp.float32))


def fox_attention(q, k, v, f_logit, b_f):
    B, S, H, Dh = q.shape
    log_f = jax.nn.log_sigmoid(f_logit.astype(jnp.float32) + b_f.astype(jnp.float32))
    c = jnp.cumsum(log_f, axis=1).transpose(0, 2, 1)
    kpos = jnp.arange(S)
    scale = Dh ** -0.5

    def block(i):
        start = i * Q_BLOCK
        qb = lax.dynamic_slice_in_dim(q, start, Q_BLOCK, axis=1)
        cb = lax.dynamic_slice_in_dim(c, start, Q_BLOCK, axis=2)
        s = jnp.einsum('bqhd,bkhd->bhqk', qb, k).astype(jnp.float32) * scale
        qpos = start + jnp.arange(Q_BLOCK)
        causal = kpos[None, :] <= qpos[:, None]
        s = jnp.where(causal, s + cb[..., :, None] - c[..., None, :], NEG_INF)
        p = jax.nn.softmax(s, axis=-1)
        return jnp.einsum('bhqk,bkhd->bqhd', p.astype(v.dtype), v)

    out = lax.map(block, jnp.arange(S // Q_BLOCK))
    return out.transpose(1, 0, 2, 3, 4).reshape(B, S, H, Dh)


def diff_attention(q, k, v, slopes, lam, norm_g, lam_init):
    B, S, H, Dh = v.shape
    kpos = jnp.arange(S)
    scale = DIFF_QK_DIM ** -0.5

    def block(i):
        start = i * Q_BLOCK
        qb = lax.dynamic_slice_in_dim(q, start, Q_BLOCK, axis=1)
        s = jnp.einsum('bqhcd,bkhcd->bhcqk', qb, k).astype(jnp.float32) * scale
        qpos = start + jnp.arange(Q_BLOCK)
        dist = qpos[:, None] - kpos[None, :]
        alibi = -slopes[:, None, None, None] * dist
        s = jnp.where(dist >= 0, s + alibi, NEG_INF)
        p = jax.nn.softmax(s, axis=-1)
        w = p[:, :, 0] - lam * p[:, :, 1]
        return jnp.einsum('bhqk,bkhd->bqhd', w.astype(v.dtype), v)

    out = lax.map(block, jnp.arange(S // Q_BLOCK))
    out = out.transpose(1, 0, 2, 3, 4).reshape(B, S, H, Dh)
    return (rms_norm(out, norm_g) * (1.0 - lam_init)).astype(v.dtype)


def strided_window_attention(q, k, v, slopes, window, dil):
    B, S, H, Dh = q.shape
    n_back = window // dil
    L = -(-S // dil)
    L = -(-L // Q_BLOCK) * Q_BLOCK
    Sp = L * dil
    nb = L // Q_BLOCK

    def strided(a):
        a = jnp.pad(a, ((0, 0), (0, Sp - S), (0, 0), (0, 0)))
        return a.reshape(B, L, dil, H, Dh).transpose(0, 2, 1, 3, 4).reshape(B, dil, nb, Q_BLOCK, H, Dh)

    def with_prev(a):
        prev = jnp.pad(a, ((0, 0), (0, 0), (1, 0), (0, 0), (0, 0), (0, 0)))[:, :, :-1]
        return jnp.concatenate([prev, a], axis=3)

    qs = strided(q)
    kb = with_prev(strided(k))
    vb = with_prev(strided(v))
    s = jnp.einsum('brnqhd,brnkhd->brnhqk', qs, kb).astype(jnp.float32) * (Dh ** -0.5)
    qi = jnp.arange(Q_BLOCK)
    ki = jnp.arange(2 * Q_BLOCK) - Q_BLOCK
    delta = qi[:, None] - ki[None, :]
    key_pos = jnp.arange(nb)[:, None] * Q_BLOCK + ki[None, :]
    valid = ((delta >= 0) & (delta <= n_back))[None] & (key_pos >= 0)[:, None, :]
    alibi = -slopes[:, None, None] * (dil * delta)
    s = jnp.where(valid[:, None], s + alibi, NEG_INF)
    m = jnp.max(s, axis=-1, keepdims=True)
    e = jnp.exp(s - m)
    den = jnp.sum(e, axis=-1, keepdims=True)
    o = jnp.einsum('brnhqk,brnkhd->brnqhd', (e / den).astype(v.dtype), vb)
    lse = (m + jnp.log(den))[..., 0]
    o = o.reshape(B, dil, L, H, Dh).transpose(0, 2, 1, 3, 4).reshape(B, Sp, H, Dh)[:, :S]
    lse = lse.transpose(0, 1, 2, 4, 3).reshape(B, dil, L, H).transpose(0, 2, 1, 3).reshape(B, Sp, H)[:, :S]
    return o, lse


def dilated_mixture_attention(q, k, v, slopes):
    outs, lses = [], []
    for window, dil in DIL_CONFIGS:
        o, lse = strided_window_attention(q, k, v, slopes, window, dil)
        outs.append(o)
        lses.append(lse)
    wts = jax.nn.softmax(jnp.stack(lses, 0), axis=0)
    out = jnp.einsum('cbsh,cbshd->bshd', wts, jnp.stack(outs, 0).astype(jnp.float32))
    return out.astype(v.dtype)


def memory_cross_attention(x, mem, w_q, w_kv, w_o):
    B, S, _ = x.shape
    M = mem.shape[1]
    q = (x @ w_q).reshape(B, S, MEM_HEADS, MEM_HEAD_DIM)
    k, v = jnp.split(mem @ w_kv, 2, axis=-1)
    k = k.reshape(B, M, MEM_HEADS, MEM_HEAD_DIM)
    v = v.reshape(B, M, MEM_HEADS, MEM_HEAD_DIM)
    s = jnp.einsum('bqhd,bkhd->bhqk', q, k).astype(jnp.float32) * (MEM_HEAD_DIM ** -0.5)
    p = jax.nn.softmax(s, axis=-1)
    o = jnp.einsum('bhqk,bkhd->bqhd', p.astype(v.dtype), v).reshape(B, S, D_MODEL)
    return o @ w_o


def conv_glu_ffn(x, w_up, conv_w, conv_b, w_down):
    u = x @ w_up
    u = lax.conv_general_dilated(u, conv_w, window_strides=(1,), padding=((CONV_WIDTH - 1, 0),),
                                 dimension_numbers=('NWC', 'WIO', 'NWC'),
                                 feature_group_count=2 * D_FF) + conv_b
    g, val = jnp.split(u, 2, axis=-1)
    return (jax.nn.gelu(g) * val) @ w_down


def setup_inputs(seed: int = 0) -> dict:
    key = jax.random.key(seed)
    ks = jax.random.split(key, 24)
    f32 = jnp.float32
    nrm = lambda k, shape, scale: jax.random.normal(k, shape, f32) * scale
    gain = lambda k, shape: 1.0 + 0.02 * jax.random.normal(k, shape, f32)
    return {
        'x': jax.random.normal(ks[0], (BATCH, SEQ, D_MODEL), f32),
        'mem': jax.random.normal(ks[1], (BATCH, MEM_TOKENS, D_MODEL), f32),
        'w_in': nrm(ks[2], (DEPTH, D_MODEL, IN_COLS), D_MODEL ** -0.5),
        'b_f': FORGET_BIAS_INIT + nrm(ks[3], (DEPTH, FOX_HEADS), 0.1),
        'lambda_q1': nrm(ks[4], (DEPTH, DIFF_QK_DIM), 0.1),
        'lambda_k1': nrm(ks[5], (DEPTH, DIFF_QK_DIM), 0.1),
        'lambda_q2': nrm(ks[6], (DEPTH, DIFF_QK_DIM), 0.1),
        'lambda_k2': nrm(ks[7], (DEPTH, DIFF_QK_DIM), 0.1),
        'diff_norm_g': gain(ks[8], (DEPTH, HEAD_DIM)),
        'w_o': nrm(ks[9], (DEPTH, MIX_W, D_MODEL), MIX_W ** -0.5 * BETA),
        'ln1_g': gain(ks[10], (DEPTH, D_MODEL)),
        'ln1_b': nrm(ks[11], (DEPTH, D_MODEL), 0.02),
        'w_mq': nrm(ks[12], (DEPTH, D_MODEL, D_MODEL), D_MODEL ** -0.5),
        'w_mkv': nrm(ks[13], (DEPTH, D_MODEL, 2 * D_MODEL), D_MODEL ** -0.5),
        'w_mo': nrm(ks[14], (DEPTH, D_MODEL, D_MODEL), D_MODEL ** -0.5 * BETA),
        'ln2_g': gain(ks[15], (DEPTH, D_MODEL)),
        'ln2_b': nrm(ks[16], (DEPTH, D_MODEL), 0.02),
        'w_up': nrm(ks[17], (DEPTH, D_MODEL, 2 * D_FF), D_MODEL ** -0.5),
        'conv_w': nrm(ks[18], (DEPTH, CONV_WIDTH, 1, 2 * D_FF), CONV_WIDTH ** -0.5),
        'conv_b': nrm(ks[19], (DEPTH, 2 * D_FF), 0.02),
        'w_down': nrm(ks[20], (DEPTH, D_FF, D_MODEL), D_FF ** -0.5 * BETA),
        'ln3_g': gain(ks[21], (DEPTH, D_MODEL)),
        'ln3_b': nrm(ks[22], (DEPTH, D_MODEL), 0.02),
    }


def reference(x, mem, w_in, b_f, lambda_q1, lambda_k1, lambda_q2, lambda_k2, diff_norm_g, w_o,
              ln1_g, ln1_b, w_mq, w_mkv, w_mo, ln2_g, ln2_b, w_up, conv_w, conv_b, w_down,
              ln3_g, ln3_b):
    B, S, _ = x.shape
    diff_slopes, dil_slopes = alibi_slopes()
    widths = [FOX_W, FOX_W, FOX_W, FOX_HEADS, DIFF_W, DIFF_W, DIFF_W, DIL_W, DIL_W, DIL_W]
    split_points = [int(p) for p in np.cumsum(widths)[:-1]]
    for l in range(DEPTH):
        h = x @ w_in[l]
        fq, fk, fv, ff, dq, dk, dv, cq, ck, cv = jnp.split(h, split_points, axis=-1)
        heads = lambda a, n: a.reshape(B, S, n, HEAD_DIM)
        fox_out = fox_attention(heads(fq, FOX_HEADS), heads(fk, FOX_HEADS), heads(fv, FOX_HEADS), ff, b_f[l])
        lam_init = 0.8 - 0.6 * math.exp(-0.3 * l)
        lam = (jnp.exp(jnp.sum(lambda_q1[l].astype(jnp.float32) * lambda_k1[l].astype(jnp.float32)))
               - jnp.exp(jnp.sum(lambda_q2[l].astype(jnp.float32) * lambda_k2[l].astype(jnp.float32)))
               + lam_init)
        diff_out = diff_attention(dq.reshape(B, S, DIFF_HEADS, 2, DIFF_QK_DIM),
                                  dk.reshape(B, S, DIFF_HEADS, 2, DIFF_QK_DIM),
                                  heads(dv, DIFF_HEADS), diff_slopes, lam, diff_norm_g[l], lam_init)
        dil_out = dilated_mixture_attention(heads(cq, DIL_HEADS), heads(ck, DIL_HEADS), heads(cv, DIL_HEADS), dil_slopes)
        mix = jnp.concatenate([fox_out, diff_out, dil_out], axis=2).reshape(B, S, MIX_W) @ w_o[l]
        x = layer_norm(ALPHA * x + mix, ln1_g[l], ln1_b[l])
        x = layer_norm(ALPHA * x + memory_cross_attention(x, mem, w_mq[l], w_mkv[l], w_mo[l]), ln2_g[l], ln2_b[l])
        x = layer_norm(ALPHA * x + conv_glu_ffn(x, w_up[l], conv_w[l], conv_b[l], w_down[l]), ln3_g[l], ln3_b[l])
    return x
```

```python
import functools
import math

import numpy as np
import jax
import jax.numpy as jnp
from jax import lax
from jax.experimental import pallas as pl
from jax.experimental.pallas import tpu as pltpu

F32 = jnp.float32
BF16 = jnp.bfloat16

D_MODEL = 1024
HEAD_DIM = 64
FOX_HEADS = 4
DIFF_HEADS = 4
DIL_HEADS = 8
FOX_W = FOX_HEADS * HEAD_DIM
DIFF_W = DIFF_HEADS * HEAD_DIM
DIL_W = DIL_HEADS * HEAD_DIM
MIX_W = FOX_W + DIFF_W + DIL_W
DIFF_QK_DIM = HEAD_DIM // 2
DIL_CONFIGS = ((128, 1), (512, 4), (2048, 16))
DIL_BLOCK = 128
MEM_HEADS = 4
MEM_HEAD_DIM = D_MODEL // MEM_HEADS
D_FF = 2816
LN_EPS = 1e-5
NEG_INF = -1e30

LANES = 128
VMEM_LIMIT = 56 * 1024 * 1024

ROW_TILE = 512
IN_CHUNK = 512
ATT_TILE = 256
FF_CHUNK = 256
N_FF_CHUNKS = D_FF // FF_CHUNK
PREP_BLOCK = 256

COL_FQ, COL_FK, COL_FV, COL_DQ, COL_DK, COL_DV = 0, 1, 2, 3, 4, 5
COL_CQ, COL_CK, COL_CV = 12, 16, 20
MAIN_COLS = 3 * MIX_W


def _alibi_slopes():
    n = DIFF_HEADS + DIL_HEADS
    slopes = 2.0 ** (-8.0 * np.arange(1, n + 1) / n)
    stride = n // DIFF_HEADS
    diff_sel = np.arange(0, n, stride)[:DIFF_HEADS]
    dil_sel = np.setdiff1d(np.arange(n), diff_sel)
    return (np.asarray(slopes[diff_sel], np.float32), np.asarray(slopes[dil_sel], np.float32))


def _params(*sem):
    return pltpu.CompilerParams(dimension_semantics=sem, vmem_limit_bytes=VMEM_LIMIT)


def _layer_norm(y, g, b):
    mu = jnp.mean(y, axis=-1, keepdims=True)
    d = y - mu
    var = jnp.mean(d * d, axis=-1, keepdims=True)
    return d * lax.rsqrt(var + LN_EPS) * g + b


def _dot_nt(a, b):
    return lax.dot_general(a, b, (((1,), (1,)), ((), ())), preferred_element_type=F32)


def _online_update(s, m, l, acc, v_tile):
    m_new = jnp.maximum(m, jnp.max(s, axis=-1, keepdims=True))
    a = jnp.exp(m - m_new)
    p = jnp.exp(s - m_new)
    l = a * l + jnp.sum(p, axis=-1, keepdims=True)
    acc = a * acc + jnp.dot(p.astype(BF16), v_tile, preferred_element_type=F32)
    return m_new, l, acc


def _inproj_kernel(x_ref, w_ref, wf_ref, h_ref, f_ref):
    xb = x_ref[...].astype(BF16)
    for n in range(0, MAIN_COLS, IN_CHUNK):
        h_ref[:, n:n + IN_CHUNK] = jnp.dot(
            xb, w_ref[:, n:n + IN_CHUNK], preferred_element_type=F32).astype(BF16)
    f_ref[...] = jnp.dot(xb, wf_ref[...], preferred_element_type=F32)


def _inproj(x2, w_main, w_ff):
    m = x2.shape[0]
    return pl.pallas_call(
        _inproj_kernel,
        grid=(m // ROW_TILE,),
        in_specs=[
            pl.BlockSpec((ROW_TILE, D_MODEL), lambda i: (i, 0)),
            pl.BlockSpec((D_MODEL, MAIN_COLS), lambda i: (0, 0)),
            pl.BlockSpec((D_MODEL, LANES), lambda i: (0, 0)),
        ],
        out_specs=[
            pl.BlockSpec((ROW_TILE, MAIN_COLS), lambda i: (i, 0)),
            pl.BlockSpec((ROW_TILE, LANES), lambda i: (i, 0)),
        ],
        out_shape=[
            jax.ShapeDtypeStruct((m, MAIN_COLS), BF16),
            jax.ShapeDtypeStruct((m, LANES), F32),
        ],
        compiler_params=_params("arbitrary"),
        name="inproj",
    )(x2, w_main, w_ff)


def _fox_prep_kernel(f_ref, bf_ref, ccol_ref, crow_ref, *, seq):
    r = lax.broadcasted_iota(jnp.int32, (PREP_BLOCK, PREP_BLOCK), 0)
    c = lax.broadcasted_iota(jnp.int32, (PREP_BLOCK, PREP_BLOCK), 1)
    tri = jnp.where(c <= r, 1.0, 0.0).astype(BF16)
    offset = jnp.zeros((1, LANES), F32)
    for blk in range(seq // PREP_BLOCK):
        rows = slice(blk * PREP_BLOCK, (blk + 1) * PREP_BLOCK)
        x = f_ref[0, rows, :] + bf_ref[...]
        logf = jnp.minimum(x, 0.0) - jnp.log1p(jnp.exp(-jnp.abs(x)))
        hi = logf.astype(BF16)
        r1 = logf - hi.astype(F32)
        mid = r1.astype(BF16)
        lo = (r1 - mid.astype(F32)).astype(BF16)
        cs = (jnp.dot(tri, hi, preferred_element_type=F32)
              + jnp.dot(tri, mid, preferred_element_type=F32)
              + jnp.dot(tri, lo, preferred_element_type=F32))
        cblk = cs + offset
        ccol_ref[0, rows, :] = cblk
        crow_ref[0, :, rows] = cblk.T[0:8, :]
        offset = cblk[PREP_BLOCK - 1:PREP_BLOCK, :]


def _fox_prep(f3, bf_row):
    b, seq, _ = f3.shape
    return pl.pallas_call(
        functools.partial(_fox_prep_kernel, seq=seq),
        grid=(b,),
        in_specs=[
            pl.BlockSpec((1, seq, LANES), lambda i: (i, 0, 0)),
            pl.BlockSpec((1, LANES), lambda i: (0, 0)),
        ],
        out_specs=[
            pl.BlockSpec((1, seq, LANES), lambda i: (i, 0, 0)),
            pl.BlockSpec((1, 8, seq), lambda i: (i, 0, 0)),
        ],
        out_shape=[
            jax.ShapeDtypeStruct((b, seq, LANES), F32),
            jax.ShapeDtypeStruct((b, 8, seq), F32),
        ],
        compiler_params=_params("arbitrary"),
        name="fox_prep",
    )(f3, bf_row)


def _fox_kernel(q_ref, k_ref, v_ref, ccol_ref, crow_ref, o_ref):
    t = ATT_TILE
    qi = pl.program_id(1)
    lane = lax.broadcasted_iota(jnp.int32, (1, LANES), 1)
    r = lax.broadcasted_iota(jnp.int32, (t, t), 0)
    c = lax.broadcasted_iota(jnp.int32, (t, t), 1)
    causal = c <= r
    ccol = ccol_ref[0]
    for pp in range(FOX_HEADS // 2):
        cols = slice(LANES * pp, LANES * (pp + 1))
        qp = q_ref[0, :, cols].astype(F32)
        res = []
        for hh in range(2):
            head = 2 * pp + hh
            half = (lane >= HEAD_DIM * hh) & (lane < HEAD_DIM * (hh + 1))
            qh = (jnp.where(half, qp, 0.0) * (HEAD_DIM ** -0.5)).astype(BF16)
            cq = ccol[:, head:head + 1]

            def tile(j, carry, masked, qh=qh, cq=cq, head=head, cols=cols):
                m, l, acc = carry
                k0 = pl.multiple_of(j * t, t)
                kt = k_ref[0, pl.ds(k0, t), cols]
                vt = v_ref[0, pl.ds(k0, t), cols]
                ck = crow_ref[0, head:head + 1, pl.ds(k0, t)]
                s = _dot_nt(qh, kt) + (cq - ck)
                if masked:
                    s = jnp.where(causal, s, NEG_INF)
                return _online_update(s, m, l, acc, vt)

            init = (jnp.full((t, 1), NEG_INF, F32), jnp.zeros((t, 1), F32),
                    jnp.zeros((t, LANES), F32))
            carry = lax.fori_loop(0, qi, functools.partial(tile, masked=False), init)
            m, l, acc = tile(qi, carry, True)
            res.append(acc * (1.0 / l))
        o_ref[0, :, cols] = jnp.where(lane < HEAD_DIM, res[0], res[1]).astype(BF16)


def _fox_attention(h3, ccol, crow):
    b, seq, _ = h3.shape
    t = ATT_TILE
    return pl.pallas_call(
        _fox_kernel,
        grid=(b, seq // t),
        in_specs=[
            pl.BlockSpec((1, t, FOX_W), lambda i, j: (i, j, COL_FQ)),
            pl.BlockSpec((1, seq, FOX_W), lambda i, j: (i, 0, COL_FK)),
            pl.BlockSpec((1, seq, FOX_W), lambda i, j: (i, 0, COL_FV)),
            pl.BlockSpec((1, t, LANES), lambda i, j: (i, j, 0)),
            pl.BlockSpec((1, 8, seq), lambda i, j: (i, 0, 0)),
        ],
        out_specs=pl.BlockSpec((1, t, FOX_W), lambda i, j: (i, j, 0)),
        out_shape=jax.ShapeDtypeStruct((b, seq, FOX_W), BF16),
        compiler_params=_params("arbitrary", "arbitrary"),
        name="fox_attn",
    )(h3, h3, h3, ccol, crow)


def _diff_kernel(q_ref, k_ref, v_ref, lamp_ref, g_ref, o_ref, *, slopes, lam_init):
    t = ATT_TILE
    qi = pl.program_id(1)
    lane = lax.broadcasted_iota(jnp.int32, (1, LANES), 1)
    r = lax.broadcasted_iota(jnp.int32, (t, t), 0)
    c = lax.broadcasted_iota(jnp.int32, (t, t), 1)
    causal = c <= r
    rel = (r - c).astype(F32)
    lamp = lamp_ref[...]
    a1 = jnp.sum(lamp[0:1] * lamp[1:2], axis=-1, keepdims=True)
    a2 = jnp.sum(lamp[2:3] * lamp[3:4], axis=-1, keepdims=True)
    lam = jnp.exp(a1) - jnp.exp(a2) + lam_init
    scale = DIFF_QK_DIM ** -0.5
    gain = g_ref[...]
    for pp in range(DIFF_HEADS // 2):
        cols = slice(LANES * pp, LANES * (pp + 1))
        qp = q_ref[0, :, cols].astype(F32)
        res = []
        for hh in range(2):
            head = 2 * pp + hh
            nslope = -float(slopes[head])
            nrel = rel * nslope
            parts = []
            for cc in range(2):
                lo = DIFF_QK_DIM * (2 * hh + cc)
                sel = (lane >= lo) & (lane < lo + DIFF_QK_DIM)
                qm = jnp.where(sel, qp, 0.0).astype(BF16)

                def tile(j, carry, masked, qm=qm, nrel=nrel, nslope=nslope, cols=cols):
                    m, l, acc = carry
                    k0 = pl.multiple_of(j * t, t)
                    kt = k_ref[0, pl.ds(k0, t), cols]
                    vt = v_ref[0, pl.ds(k0, t), cols]
                    d0 = ((qi - j) * t).astype(F32)
                    s = _dot_nt(qm, kt) * scale + (nrel + nslope * d0)
                    if masked:
                        s = jnp.where(causal, s, NEG_INF)
                    return _online_update(s, m, l, acc, vt)

                init = (jnp.full((t, 1), NEG_INF, F32), jnp.zeros((t, 1), F32),
                        jnp.zeros((t, LANES), F32))
                carry = lax.fori_loop(0, qi, functools.partial(tile, masked=False), init)
                m, l, acc = tile(qi, carry, True)
                parts.append(acc * (1.0 / l))
            o = parts[0] - lam * parts[1]
            half = (lane >= HEAD_DIM * hh) & (lane < HEAD_DIM * (hh + 1))
            ms = jnp.sum(jnp.where(half, o * o, 0.0), axis=-1, keepdims=True) * (1.0 / HEAD_DIM)
            res.append(o * lax.rsqrt(ms + LN_EPS) * gain * (1.0 - lam_init))
        o_ref[0, :, cols] = jnp.where(lane < HEAD_DIM, res[0], res[1]).astype(BF16)


def _diff_attention(h3, lamp, gain_row, slopes, lam_init):
    b, seq, _ = h3.shape
    t = ATT_TILE
    return pl.pallas_call(
        functools.partial(_diff_kernel, slopes=slopes, lam_init=lam_init),
        grid=(b, seq // t),
        in_specs=[
            pl.BlockSpec((1, t, DIFF_W), lambda i, j: (i, j, COL_DQ)),
            pl.BlockSpec((1, seq, DIFF_W), lambda i, j: (i, 0, COL_DK)),
            pl.BlockSpec((1, seq, DIFF_W), lambda i, j: (i, 0, COL_DV)),
            pl.BlockSpec((8, LANES), lambda i, j: (0, 0)),
            pl.BlockSpec((1, LANES), lambda i, j: (0, 0)),
        ],
        out_specs=pl.BlockSpec((1, t, DIFF_W), lambda i, j: (i, j, 0)),
        out_shape=jax.ShapeDtypeStruct((b, seq, DIFF_W), BF16),
        compiler_params=_params("arbitrary", "arbitrary"),
        name="diff_attn",
    )(h3, h3, h3, lamp, gain_row)


def _dil_block(qb, kw, vw, bias):
    s = _dot_nt(qb, kw) + bias
    m = jnp.max(s, axis=-1, keepdims=True)
    p = jnp.exp(s - m)
    l = jnp.sum(p, axis=-1, keepdims=True)
    acc = jnp.dot(p.astype(BF16), vw, preferred_element_type=F32)
    return m, l, acc


def _dil_kernel(q_ref, k_ref, v_ref, slope_ref, o_ref,
                qf, kf, vf, qs, ks, vs, m_st, l_st, a_st, *, seq):
    blk = DIL_BLOCK
    lane = lax.broadcasted_iota(jnp.int32, (1, LANES), 1)
    even = lane < HEAD_DIM
    slope = slope_ref[0]
    nslope_e = -slope[:, 0:1]
    nslope_o = -slope[:, HEAD_DIM:HEAD_DIM + 1]
    qi = lax.broadcasted_iota(jnp.int32, (blk, 2 * blk), 0)
    ki = lax.broadcasted_iota(jnp.int32, (blk, 2 * blk), 1) - blk
    delta = qi - ki
    valid = (delta >= 0) & (delta <= blk)
    deltaf = delta.astype(F32)

    qf[...] = q_ref[0].astype(F32)
    kf[...] = k_ref[0].astype(F32)
    vf[...] = v_ref[0].astype(F32)

    for _, dil in DIL_CONFIGS:
        length = seq // dil
        nblocks = length // blk
        dist = deltaf * float(dil)
        bias_e = jnp.where(valid, nslope_e * dist, NEG_INF)
        bias_o = jnp.where(valid, nslope_o * dist, NEG_INF)

        def residue(res, dil=dil, length=length, nblocks=nblocks, bias_e=bias_e, bias_o=bias_o):
            if dil == 1:
                def q_rows(a, n):
                    return q_ref[0, pl.ds(a, n), :]

                def k_rows(a, n):
                    return k_ref[0, pl.ds(a, n), :]

                def v_rows(a, n):
                    return v_ref[0, pl.ds(a, n), :]
            else:
                qs[0:length, :] = qf[pl.ds(res, length, stride=dil), :].astype(BF16)
                ks[0:length, :] = kf[pl.ds(res, length, stride=dil), :].astype(BF16)
                vs[0:length, :] = vf[pl.ds(res, length, stride=dil), :].astype(BF16)

                def q_rows(a, n):
                    return qs[pl.ds(a, n), :]

                def k_rows(a, n):
                    return ks[pl.ds(a, n), :]

                def v_rows(a, n):
                    return vs[pl.ds(a, n), :]

            def do_block(n, first):
                if first:
                    q0 = 0
                    kw = k_rows(0, blk)
                    vw = v_rows(0, blk)
                    be, bo = bias_e[:, blk:], bias_o[:, blk:]
                else:
                    q0 = pl.multiple_of(n * blk, blk)
                    k0 = pl.multiple_of((n - 1) * blk, blk)
                    kw = k_rows(k0, 2 * blk)
                    vw = v_rows(k0, 2 * blk)
                    be, bo = bias_e, bias_o
                qb = q_rows(q0, blk).astype(F32) * (HEAD_DIM ** -0.5)
                qe = jnp.where(even, qb, 0.0).astype(BF16)
                qo = jnp.where(even, 0.0, qb).astype(BF16)
                m_e, l_e, a_e = _dil_block(qe, kw, vw, be)
                m_o, l_o, a_o = _dil_block(qo, kw, vw, bo)
                m_b = jnp.where(even, m_e, m_o)
                l_b = jnp.where(even, l_e, l_o)
                a_b = jnp.where(even, a_e, a_o)
                if dil == 1:
                    rows = pl.ds(q0, blk)
                    m_st[rows, :] = m_b
                    l_st[rows, :] = l_b
                    a_st[rows, :] = a_b
                else:
                    rows = pl.ds(q0 * dil + res, blk, stride=dil)
                    m_old = m_st[rows, :]
                    m_new = jnp.maximum(m_old, m_b)
                    w_old = jnp.exp(m_old - m_new)
                    w_blk = jnp.exp(m_b - m_new)
                    m_st[rows, :] = m_new
                    l_st[rows, :] = w_old * l_st[rows, :] + w_blk * l_b
                    a_st[rows, :] = w_old * a_st[rows, :] + w_blk * a_b

            do_block(0, True)
            if nblocks > 1:
                def body(n, carry):
                    do_block(n, False)
                    return carry
                lax.fori_loop(1, nblocks, body, 0)

        if dil == 1:
            residue(0)
        else:
            def res_body(res, carry, residue=residue):
                residue(res)
                return carry
            lax.fori_loop(0, dil, res_body, 0)

    o_ref[0] = (a_st[...] * (1.0 / l_st[...])).astype(BF16)


def _dil_attention(h3, slope_pairs):
    b, seq, _ = h3.shape
    npairs = DIL_HEADS // 2
    max_strided = seq // DIL_CONFIGS[1][1]
    return pl.pallas_call(
        functools.partial(_dil_kernel, seq=seq),
        grid=(b, npairs),
        in_specs=[
            pl.BlockSpec((1, seq, LANES), lambda i, p: (i, 0, COL_CQ + p)),
            pl.BlockSpec((1, seq, LANES), lambda i, p: (i, 0, COL_CK + p)),
            pl.BlockSpec((1, seq, LANES), lambda i, p: (i, 0, COL_CV + p)),
            pl.BlockSpec((1, 1, LANES), lambda i, p: (p, 0, 0)),
        ],
        out_specs=pl.BlockSpec((1, seq, LANES), lambda i, p: (i, 0, p)),
        out_shape=jax.ShapeDtypeStruct((b, seq, DIL_W), BF16),
        scratch_shapes=[
            pltpu.VMEM((seq, LANES), F32), pltpu.VMEM((seq, LANES), F32),
            pltpu.VMEM((seq, LANES), F32),
            pltpu.VMEM((max_strided, LANES), BF16), pltpu.VMEM((max_strided, LANES), BF16),
            pltpu.VMEM((max_strided, LANES), BF16),
            pltpu.VMEM((seq, LANES), F32), pltpu.VMEM((seq, LANES), F32),
            pltpu.VMEM((seq, LANES), F32),
        ],
        compiler_params=_params("arbitrary", "arbitrary"),
        name="dil_attn",
    )(h3, h3, h3, slope_pairs)


def _outproj_kernel(x_ref, fox_ref, diff_ref, dil_ref, w_ref, g_ref, b_ref, o_ref, *, alpha):
    mix = (jnp.dot(fox_ref[...], w_ref[0:FOX_W, :], preferred_element_type=F32)
           + jnp.dot(diff_ref[...], w_ref[FOX_W:FOX_W + DIFF_W, :], preferred_element_type=F32)
           + jnp.dot(dil_ref[...], w_ref[FOX_W + DIFF_W:, :], preferred_element_type=F32))
    y = alpha * x_ref[...] + mix
    o_ref[...] = _layer_norm(y, g_ref[...], b_ref[...])


def _outproj_ln(x2, fox, diff, dil, w_o, g, b, alpha):
    m = x2.shape[0]
    row = lambda w: pl.BlockSpec((ROW_TILE, w), lambda i: (i, 0))
    const = lambda r, w: pl.BlockSpec((r, w), lambda i: (0, 0))
    return pl.pallas_call(
        functools.partial(_outproj_kernel, alpha=alpha),
        grid=(m // ROW_TILE,),
        in_specs=[row(D_MODEL), row(FOX_W), row(DIFF_W), row(DIL_W),
                  const(MIX_W, D_MODEL), const(1, D_MODEL), const(1, D_MODEL)],
        out_specs=row(D_MODEL),
        out_shape=jax.ShapeDtypeStruct((m, D_MODEL), F32),
        compiler_params=_params("arbitrary"),
        name="outproj_ln",
    )(x2, fox, diff, dil, w_o, g, b)


def _memkv_kernel(m_ref, w_ref, o_ref):
    o_ref[0] = jnp.dot(m_ref[0].astype(BF16), w_ref[...],
                       preferred_element_type=F32).astype(BF16)


def _memkv(mem, w_kv):
    b, mt, _ = mem.shape
    return pl.pallas_call(
        _memkv_kernel,
        grid=(b,),
        in_specs=[pl.BlockSpec((1, mt, D_MODEL), lambda i: (i, 0, 0)),
                  pl.BlockSpec((D_MODEL, 2 * D_MODEL), lambda i: (0, 0))],
        out_specs=pl.BlockSpec((1, mt, 2 * D_MODEL), lambda i: (i, 0, 0)),
        out_shape=jax.ShapeDtypeStruct((b, mt, 2 * D_MODEL), BF16),
        compiler_params=_params("arbitrary"),
        name="memkv",
    )(mem, w_kv)


def _memattn_kernel(x_ref, kv_ref, wq_ref, wo_ref, g_ref, b_ref, o_ref, *, alpha):
    x = x_ref[0]
    q = jnp.dot(x.astype(BF16), wq_ref[...], preferred_element_type=F32)
    q = (q * (MEM_HEAD_DIM ** -0.5)).astype(BF16)
    y = alpha * x
    for h in range(MEM_HEADS):
        cols = slice(MEM_HEAD_DIM * h, MEM_HEAD_DIM * (h + 1))
        vcols = slice(D_MODEL + MEM_HEAD_DIM * h, D_MODEL + MEM_HEAD_DIM * (h + 1))
        s = _dot_nt(q[:, cols], kv_ref[0, :, cols])
        m = jnp.max(s, axis=-1, keepdims=True)
        p = jnp.exp(s - m)
        l = jnp.sum(p, axis=-1, keepdims=True)
        o = jnp.dot(p.astype(BF16), kv_ref[0, :, vcols], preferred_element_type=F32) * (1.0 / l)
        y = y + jnp.dot(o.astype(BF16), wo_ref[cols, :], preferred_element_type=F32)
    o_ref[0] = _layer_norm(y, g_ref[...], b_ref[...])


def _memattn_ln(x3, kv, w_q, w_o, g, b, alpha):
    bsz, seq, _ = x3.shape
    mt = kv.shape[1]
    const = lambda r, w: pl.BlockSpec((r, w), lambda i, j: (0, 0))
    return pl.pallas_call(
        functools.partial(_memattn_kernel, alpha=alpha),
        grid=(bsz, seq // ROW_TILE),
        in_specs=[
            pl.BlockSpec((1, ROW_TILE, D_MODEL), lambda i, j: (i, j, 0)),
            pl.BlockSpec((1, mt, 2 * D_MODEL), lambda i, j: (i, 0, 0)),
            const(D_MODEL, D_MODEL), const(D_MODEL, D_MODEL),
            const(1, D_MODEL), const(1, D_MODEL),
        ],
        out_specs=pl.BlockSpec((1, ROW_TILE, D_MODEL), lambda i, j: (i, j, 0)),
        out_shape=jax.ShapeDtypeStruct((bsz, seq, D_MODEL), F32),
        compiler_params=_params("arbitrary", "arbitrary"),
        name="memattn_ln",
    )(x3, kv, w_q, w_o, g, b)


def _ffn_kernel(x_ref, wup_ref, cp_ref, wdn_ref, g_ref, b_ref, o_ref, carry_ref, acc_ref,
                *, alpha):
    tm = ROW_TILE
    width = 2 * FF_CHUNK

    @pl.when(pl.program_id(1) == 0)
    def _():
        carry_ref[...] = jnp.zeros_like(carry_ref)

    x = x_ref[0]
    xb = x.astype(BF16)
    row = lax.broadcasted_iota(jnp.int32, (tm, width), 0)
    acc_ref[...] = alpha * x

    def chunk(c, carry):
        u = jnp.dot(xb, wup_ref[c], preferred_element_type=F32)
        cp = cp_ref[c]
        prev = carry_ref[c]
        carry_ref[c] = u[tm - 8:tm, :]
        u1 = jnp.where(row == 0, prev[7:8], pltpu.roll(u, 1, 0))
        u2 = jnp.where(row == 0, prev[6:7],
                       jnp.where(row == 1, prev[7:8], pltpu.roll(u, 2, 0)))
        y = cp[0:1] * u2 + cp[1:2] * u1 + cp[2:3] * u + cp[3:4]
        h = (jax.nn.gelu(y[:, :FF_CHUNK]) * y[:, FF_CHUNK:]).astype(BF16)
        acc_ref[...] += jnp.dot(h, wdn_ref[c], preferred_element_type=F32)
        return carry

    lax.fori_loop(0, N_FF_CHUNKS, chunk, 0)
    o_ref[0] = _layer_norm(acc_ref[...], g_ref[...], b_ref[...])


def _ffn_ln(x3, w_up, cp, w_dn, g, b, alpha):
    bsz, seq, _ = x3.shape
    width = 2 * FF_CHUNK
    const2 = lambda r, w: pl.BlockSpec((r, w), lambda i, j: (0, 0))
    const3 = lambda a, r, w: pl.BlockSpec((a, r, w), lambda i, j: (0, 0, 0))
    return pl.pallas_call(
        functools.partial(_ffn_kernel, alpha=alpha),
        grid=(bsz, seq // ROW_TILE),
        in_specs=[
            pl.BlockSpec((1, ROW_TILE, D_MODEL), lambda i, j: (i, j, 0)),
            const3(N_FF_CHUNKS, D_MODEL, width),
            const3(N_FF_CHUNKS, 8, width),
            const3(N_FF_CHUNKS, FF_CHUNK, D_MODEL),
            const2(1, D_MODEL), const2(1, D_MODEL),
        ],
        out_specs=pl.BlockSpec((1, ROW_TILE, D_MODEL), lambda i, j: (i, j, 0)),
        out_shape=jax.ShapeDtypeStruct((bsz, seq, D_MODEL), F32),
        scratch_shapes=[pltpu.VMEM((N_FF_CHUNKS, 8, width), F32),
                        pltpu.VMEM((ROW_TILE, D_MODEL), F32)],
        compiler_params=_params("arbitrary", "arbitrary"),
        name="ffn_ln",
    )(x3, w_up, cp, w_dn, g, b)


def _pad_lanes(v):
    return jnp.pad(v.astype(F32), (0, LANES - v.shape[0])).reshape(1, LANES)


def kernel(x, mem, w_in, b_f, lambda_q1, lambda_k1, lambda_q2, lambda_k2, diff_norm_g, w_o,
           ln1_g, ln1_b, w_mq, w_mkv, w_mo, ln2_g, ln2_b, w_up, conv_w, conv_b, w_down,
           ln3_g, ln3_b):
    bsz, seq, _ = x.shape
    depth = w_in.shape[0]
    alpha = (2 * depth) ** 0.25
    diff_slopes, dil_slopes = _alibi_slopes()
    slope_pairs = jnp.asarray(
        np.repeat(dil_slopes.reshape(DIL_HEADS // 2, 2), HEAD_DIM, axis=1)
        .reshape(DIL_HEADS // 2, 1, LANES))
    ff_start = 3 * FOX_W
    row2 = lambda v: v.astype(F32).reshape(1, D_MODEL)

    x2 = x.reshape(bsz * seq, D_MODEL)
    for l in range(depth):
        lam_init = 0.8 - 0.6 * math.exp(-0.3 * l)
        w_main = jnp.concatenate(
            [w_in[l][:, :ff_start], w_in[l][:, ff_start + FOX_HEADS:]], axis=1).astype(BF16)
        w_ff = jnp.pad(w_in[l][:, ff_start:ff_start + FOX_HEADS],
                       ((0, 0), (0, LANES - FOX_HEADS))).astype(BF16)
        h2, f2 = _inproj(x2, w_main, w_ff)
        h3 = h2.reshape(bsz, seq, MAIN_COLS)
        ccol, crow = _fox_prep(f2.reshape(bsz, seq, LANES), _pad_lanes(b_f[l]))
        fox = _fox_attention(h3, ccol, crow)
        lamp = jnp.zeros((8, LANES), F32)
        for i, v in enumerate((lambda_q1, lambda_k1, lambda_q2, lambda_k2)):
            lamp = lamp.at[i, :DIFF_QK_DIM].set(v[l].astype(F32))
        gain_row = jnp.tile(diff_norm_g[l].astype(F32), 2).reshape(1, LANES)
        diff = _diff_attention(h3, lamp, gain_row, diff_slopes, lam_init)
        dil = _dil_attention(h3, slope_pairs)
        x2 = _outproj_ln(x2, fox.reshape(bsz * seq, FOX_W), diff.reshape(bsz * seq, DIFF_W),
                         dil.reshape(bsz * seq, DIL_W), w_o[l].astype(BF16),
                         row2(ln1_g[l]), row2(ln1_b[l]), alpha)
        kv = _memkv(mem, w_mkv[l].astype(BF16))
        x3 = _memattn_ln(x2.reshape(bsz, seq, D_MODEL), kv, w_mq[l].astype(BF16),
                         w_mo[l].astype(BF16), row2(ln2_g[l]), row2(ln2_b[l]), alpha)
        wu = w_up[l].reshape(D_MODEL, 2, N_FF_CHUNKS, FF_CHUNK).transpose(2, 0, 1, 3)
        wu = wu.reshape(N_FF_CHUNKS, D_MODEL, 2 * FF_CHUNK).astype(BF16)
        taps = jnp.concatenate([conv_w[l][:, 0, :], conv_b[l][None, :]], axis=0).astype(F32)
        taps = taps.reshape(4, 2, N_FF_CHUNKS, FF_CHUNK).transpose(2, 0, 1, 3)
        cp = jnp.pad(taps.reshape(N_FF_CHUNKS, 4, 2 * FF_CHUNK), ((0, 0), (0, 4), (0, 0)))
        wd = w_down[l].reshape(N_FF_CHUNKS, FF_CHUNK, D_MODEL).astype(BF16)
        x3 = _ffn_ln(x3, wu, cp, wd, row2(ln3_g[l]), row2(ln3_b[l]), alpha)
        x2 = x3.reshape(bsz * seq, D_MODEL)
    return x2.reshape(bsz, seq, D_MODEL)
```

```python
import functools
import math

import numpy as np
import jax
import jax.numpy as jnp
from jax import lax
from jax.experimental import pallas as pl
from jax.experimental.pallas import tpu as pltpu

F32 = jnp.float32
BF16 = jnp.bfloat16

D_MODEL = 1024
HEAD_DIM = 64
FOX_HEADS = 4
DIFF_HEADS = 4
DIL_HEADS = 8
FOX_W = FOX_HEADS * HEAD_DIM
DIFF_W = DIFF_HEADS * HEAD_DIM
DIL_W = DIL_HEADS * HEAD_DIM
MIX_W = FOX_W + DIFF_W + DIL_W
DIFF_QK_DIM = HEAD_DIM // 2
DIL_CONFIGS = ((128, 1), (512, 4), (2048, 16))
DIL_BLOCK = 128
MEM_HEADS = 4
MEM_HEAD_DIM = D_MODEL // MEM_HEADS
D_FF = 2816
LN_EPS = 1e-5
NEG_INF = -1e30

LANES = 128
VMEM_LIMIT = 56 * 1024 * 1024

ROW_TILE = 512
IN_CHUNK = 512
ATT_TILE = 256
FF_CHUNK = 256
N_FF_CHUNKS = D_FF // FF_CHUNK
PREP_BLOCK = 256

COL_FQ, COL_FK, COL_FV, COL_DQ, COL_DK, COL_DV = 0, 1, 2, 3, 4, 5
COL_CQ, COL_CK, COL_CV = 12, 16, 20
MAIN_COLS = 3 * MIX_W


def _alibi_slopes():
    n = DIFF_HEADS + DIL_HEADS
    slopes = 2.0 ** (-8.0 * np.arange(1, n + 1) / n)
    stride = n // DIFF_HEADS
    diff_sel = np.arange(0, n, stride)[:DIFF_HEADS]
    dil_sel = np.setdiff1d(np.arange(n), diff_sel)
    return (np.asarray(slopes[diff_sel], np.float32), np.asarray(slopes[dil_sel], np.float32))


def _params(*sem):
    return pltpu.CompilerParams(dimension_semantics=sem, vmem_limit_bytes=VMEM_LIMIT)


def _layer_norm(y, g, b):
    mu = jnp.mean(y, axis=-1, keepdims=True)
    d = y - mu
    var = jnp.mean(d * d, axis=-1, keepdims=True)
    return d * lax.rsqrt(var + LN_EPS) * g + b


def _dot_nt(a, b):
    return lax.dot_general(a, b, (((1,), (1,)), ((), ())), preferred_element_type=F32)


def _inproj_kernel(x_ref, w_ref, wf_ref, h_ref, f_ref):
    xb = x_ref[...].astype(BF16)
    for n in range(0, MAIN_COLS, IN_CHUNK):
        h_ref[:, n:n + IN_CHUNK] = jnp.dot(
            xb, w_ref[:, n:n + IN_CHUNK], preferred_element_type=F32).astype(BF16)
    f_ref[...] = jnp.dot(xb, wf_ref[...], preferred_element_type=F32)


def _inproj(x2, w_main, w_ff):
    m = x2.shape[0]
    return pl.pallas_call(
        _inproj_kernel,
        grid=(m // ROW_TILE,),
        in_specs=[
            pl.BlockSpec((ROW_TILE, D_MODEL), lambda i: (i, 0)),
            pl.BlockSpec((D_MODEL, MAIN_COLS), lambda i: (0, 0)),
            pl.BlockSpec((D_MODEL, LANES), lambda i: (0, 0)),
        ],
        out_specs=[
            pl.BlockSpec((ROW_TILE, MAIN_COLS), lambda i: (i, 0)),
            pl.BlockSpec((ROW_TILE, LANES), lambda i: (i, 0)),
        ],
        out_shape=[
            jax.ShapeDtypeStruct((m, MAIN_COLS), BF16),
            jax.ShapeDtypeStruct((m, LANES), F32),
        ],
        compiler_params=_params("arbitrary"),
        name="inproj",
    )(x2, w_main, w_ff)


def _fox_prep_kernel(f_ref, bf_ref, ccol_ref, crow_ref, *, seq):
    r = lax.broadcasted_iota(jnp.int32, (PREP_BLOCK, PREP_BLOCK), 0)
    c = lax.broadcasted_iota(jnp.int32, (PREP_BLOCK, PREP_BLOCK), 1)
    tri = jnp.where(c <= r, 1.0, 0.0).astype(BF16)
    offset = jnp.zeros((1, LANES), F32)
    for blk in range(seq // PREP_BLOCK):
        rows = slice(blk * PREP_BLOCK, (blk + 1) * PREP_BLOCK)
        x = f_ref[0, rows, :] + bf_ref[...]
        logf = jnp.minimum(x, 0.0) - jnp.log1p(jnp.exp(-jnp.abs(x)))
        hi = logf.astype(BF16)
        r1 = logf - hi.astype(F32)
        mid = r1.astype(BF16)
        lo = (r1 - mid.astype(F32)).astype(BF16)
        cs = (jnp.dot(tri, hi, preferred_element_type=F32)
              + jnp.dot(tri, mid, preferred_element_type=F32)
              + jnp.dot(tri, lo, preferred_element_type=F32))
        cblk = cs + offset
        ccol_ref[0, rows, :] = cblk
        crow_ref[0, :, rows] = cblk.T[0:8, :]
        offset = cblk[PREP_BLOCK - 1:PREP_BLOCK, :]


def _fox_prep(f3, bf_row):
    b, seq, _ = f3.shape
    return pl.pallas_call(
        functools.partial(_fox_prep_kernel, seq=seq),
        grid=(b,),
        in_specs=[
            pl.BlockSpec((1, seq, LANES), lambda i: (i, 0, 0)),
            pl.BlockSpec((1, LANES), lambda i: (0, 0)),
        ],
        out_specs=[
            pl.BlockSpec((1, seq, LANES), lambda i: (i, 0, 0)),
            pl.BlockSpec((1, 8, seq), lambda i: (i, 0, 0)),
        ],
        out_shape=[
            jax.ShapeDtypeStruct((b, seq, LANES), F32),
            jax.ShapeDtypeStruct((b, 8, seq), F32),
        ],
        compiler_params=_params("arbitrary"),
        name="fox_prep",
    )(f3, bf_row)


def _causal_two_phase(n_pairs, qi, score_tile, value_tile, s_sc, mp_sc, lp_sc, acc_sc):
    t = ATT_TILE
    mp_sc[...] = jnp.full(mp_sc.shape, NEG_INF, F32)

    def phase1(j, masked):
        k0 = pl.multiple_of(j * t, t)
        for pp in range(n_pairs):
            s = score_tile(pp, j, k0, masked)
            s_sc[pp, :, pl.ds(k0, t)] = s
            mp_sc[pp] = jnp.maximum(mp_sc[pp], jnp.maximum(s[:, :LANES], s[:, LANES:]))

    def phase1_body(j, carry):
        phase1(j, False)
        return carry

    lax.fori_loop(0, qi, phase1_body, 0)
    phase1(qi, True)

    lp_sc[...] = jnp.zeros(lp_sc.shape, F32)
    acc_sc[...] = jnp.zeros(acc_sc.shape, F32)
    row_max = [jnp.max(mp_sc[pp], axis=-1, keepdims=True) for pp in range(n_pairs)]

    def phase2_body(j, carry):
        k0 = pl.multiple_of(j * t, t)
        for pp in range(n_pairs):
            p = jnp.exp(s_sc[pp, :, pl.ds(k0, t)] - row_max[pp])
            lp_sc[pp] += p[:, :LANES] + p[:, LANES:]
            acc_sc[pp] += jnp.dot(p.astype(BF16), value_tile(pp, k0),
                                  preferred_element_type=F32)
        return carry

    lax.fori_loop(0, qi + 1, phase2_body, 0)
    return [(acc_sc[pp], jnp.sum(lp_sc[pp], axis=-1, keepdims=True)) for pp in range(n_pairs)]


def _attn_scratch(n_pairs, rows, seq):
    return [pltpu.VMEM((n_pairs, rows, seq), F32), pltpu.VMEM((n_pairs, rows, LANES), F32),
            pltpu.VMEM((n_pairs, rows, LANES), F32), pltpu.VMEM((n_pairs, rows, LANES), F32)]


def _fox_kernel(q_ref, k_ref, v_ref, ccol_ref, crow_ref, o_ref, s_sc, mp_sc, lp_sc, acc_sc):
    t = ATT_TILE
    n_pairs = FOX_HEADS // 2
    qi = pl.program_id(1)
    lane = lax.broadcasted_iota(jnp.int32, (1, LANES), 1)
    even = lane < HEAD_DIM
    r = lax.broadcasted_iota(jnp.int32, (t, t), 0)
    c = lax.broadcasted_iota(jnp.int32, (t, t), 1)
    causal = c <= r
    ccol = ccol_ref[0]
    cols = [slice(LANES * pp, LANES * (pp + 1)) for pp in range(n_pairs)]
    q_stack = []
    for pp in range(n_pairs):
        qp = q_ref[0, :, cols[pp]].astype(F32) * (HEAD_DIM ** -0.5)
        q_stack.append(jnp.concatenate(
            [jnp.where(even, qp, 0.0), jnp.where(even, 0.0, qp)], axis=0).astype(BF16))

    def score_tile(pp, j, k0, masked):
        s = _dot_nt(q_stack[pp], k_ref[0, pl.ds(k0, t), cols[pp]])
        parts = []
        for hh in range(2):
            head = 2 * pp + hh
            ck = crow_ref[0, head:head + 1, pl.ds(k0, t)]
            sh = s[hh * t:(hh + 1) * t] + (ccol[:, head:head + 1] - ck)
            if masked:
                sh = jnp.where(causal, sh, NEG_INF)
            parts.append(sh)
        return jnp.concatenate(parts, axis=0)

    def value_tile(pp, k0):
        return v_ref[0, pl.ds(k0, t), cols[pp]]

    res = _causal_two_phase(n_pairs, qi, score_tile, value_tile, s_sc, mp_sc, lp_sc, acc_sc)
    for pp in range(n_pairs):
        acc, l = res[pp]
        o = acc * (1.0 / l)
        o_ref[0, :, cols[pp]] = jnp.where(even, o[:t], o[t:]).astype(BF16)


def _fox_attention(h3, ccol, crow):
    b, seq, _ = h3.shape
    t = ATT_TILE
    return pl.pallas_call(
        _fox_kernel,
        grid=(b, seq // t),
        in_specs=[
            pl.BlockSpec((1, t, FOX_W), lambda i, j: (i, j, COL_FQ)),
            pl.BlockSpec((1, seq, FOX_W), lambda i, j: (i, 0, COL_FK)),
            pl.BlockSpec((1, seq, FOX_W), lambda i, j: (i, 0, COL_FV)),
            pl.BlockSpec((1, t, LANES), lambda i, j: (i, j, 0)),
            pl.BlockSpec((1, 8, seq), lambda i, j: (i, 0, 0)),
        ],
        out_specs=pl.BlockSpec((1, t, FOX_W), lambda i, j: (i, j, 0)),
        out_shape=jax.ShapeDtypeStruct((b, seq, FOX_W), BF16),
        scratch_shapes=_attn_scratch(FOX_HEADS // 2, 2 * t, seq),
        compiler_params=_params("arbitrary", "arbitrary"),
        name="fox_attn",
    )(h3, h3, h3, ccol, crow)


def _diff_kernel(q_ref, k_ref, v_ref, lamp_ref, g_ref, o_ref, s_sc, mp_sc, lp_sc, acc_sc,
                 *, slopes, lam_init):
    t = ATT_TILE
    n_pairs = DIFF_HEADS // 2
    qi = pl.program_id(1)
    lane = lax.broadcasted_iota(jnp.int32, (1, LANES), 1)
    r = lax.broadcasted_iota(jnp.int32, (t, t), 0)
    c = lax.broadcasted_iota(jnp.int32, (t, t), 1)
    causal = c <= r
    rel = (r - c).astype(F32)
    lamp = lamp_ref[...]
    a1 = jnp.sum(lamp[0:1] * lamp[1:2], axis=-1, keepdims=True)
    a2 = jnp.sum(lamp[2:3] * lamp[3:4], axis=-1, keepdims=True)
    lam = jnp.exp(a1) - jnp.exp(a2) + lam_init
    scale = DIFF_QK_DIM ** -0.5
    gain = g_ref[...]
    cols = [slice(LANES * pp, LANES * (pp + 1)) for pp in range(n_pairs)]
    nslope = [-float(s) for s in slopes]
    nrel = [rel * ns for ns in nslope]
    q_stack = []
    for pp in range(n_pairs):
        qp = q_ref[0, :, cols[pp]].astype(F32)
        groups = []
        for mi in range(4):
            sel = (lane >= DIFF_QK_DIM * mi) & (lane < DIFF_QK_DIM * (mi + 1))
            groups.append(jnp.where(sel, qp, 0.0))
        q_stack.append(jnp.concatenate(groups, axis=0).astype(BF16))

    def score_tile(pp, j, k0, masked):
        s = _dot_nt(q_stack[pp], k_ref[0, pl.ds(k0, t), cols[pp]]) * scale
        d0 = ((qi - j) * t).astype(F32)
        parts = []
        for hh in range(2):
            head = 2 * pp + hh
            bias = nrel[head] + nslope[head] * d0
            for cc in range(2):
                mi = 2 * hh + cc
                sh = s[mi * t:(mi + 1) * t] + bias
                if masked:
                    sh = jnp.where(causal, sh, NEG_INF)
                parts.append(sh)
        return jnp.concatenate(parts, axis=0)

    def value_tile(pp, k0):
        return v_ref[0, pl.ds(k0, t), cols[pp]]

    res = _causal_two_phase(n_pairs, qi, score_tile, value_tile, s_sc, mp_sc, lp_sc, acc_sc)
    for pp in range(n_pairs):
        acc, l = res[pp]
        o = acc * (1.0 / l)
        outs = []
        for hh in range(2):
            oh = o[(2 * hh) * t:(2 * hh + 1) * t] - lam * o[(2 * hh + 1) * t:(2 * hh + 2) * t]
            half = (lane >= HEAD_DIM * hh) & (lane < HEAD_DIM * (hh + 1))
            ms = jnp.sum(jnp.where(half, oh * oh, 0.0), axis=-1, keepdims=True) * (1.0 / HEAD_DIM)
            outs.append(oh * lax.rsqrt(ms + LN_EPS) * gain * (1.0 - lam_init))
        o_ref[0, :, cols[pp]] = jnp.where(lane < HEAD_DIM, outs[0], outs[1]).astype(BF16)


def _diff_attention(h3, lamp, gain_row, slopes, lam_init):
    b, seq, _ = h3.shape
    t = ATT_TILE
    return pl.pallas_call(
        functools.partial(_diff_kernel, slopes=slopes, lam_init=lam_init),
        grid=(b, seq // t),
        in_specs=[
            pl.BlockSpec((1, t, DIFF_W), lambda i, j: (i, j, COL_DQ)),
            pl.BlockSpec((1, seq, DIFF_W), lambda i, j: (i, 0, COL_DK)),
            pl.BlockSpec((1, seq, DIFF_W), lambda i, j: (i, 0, COL_DV)),
            pl.BlockSpec((8, LANES), lambda i, j: (0, 0)),
            pl.BlockSpec((1, LANES), lambda i, j: (0, 0)),
        ],
        out_specs=pl.BlockSpec((1, t, DIFF_W), lambda i, j: (i, j, 0)),
        out_shape=jax.ShapeDtypeStruct((b, seq, DIFF_W), BF16),
        scratch_shapes=_attn_scratch(DIFF_HEADS // 2, 4 * t, seq),
        compiler_params=_params("arbitrary", "arbitrary"),
        name="diff_attn",
    )(h3, h3, h3, lamp, gain_row)


def _dil_kernel(q_ref, k_ref, v_ref, slope_ref, o_ref,
                qf, kf, vf, kp, vp, m_st, l_st, a_st, *, seq):
    blk = DIL_BLOCK
    nblk = seq // blk
    lane = lax.broadcasted_iota(jnp.int32, (1, LANES), 1)
    even = lane < HEAD_DIM
    slope = slope_ref[0]
    nslope_e = -slope[:, 0:1]
    nslope_o = -slope[:, HEAD_DIM:HEAD_DIM + 1]
    qi = lax.broadcasted_iota(jnp.int32, (blk, 2 * blk), 0)
    ki = lax.broadcasted_iota(jnp.int32, (blk, 2 * blk), 1) - blk
    delta = qi - ki
    valid = (delta >= 0) & (delta <= blk)
    deltaf = delta.astype(F32)

    qf[...] = q_ref[0].astype(F32) * (HEAD_DIM ** -0.5)
    kf[...] = k_ref[0].astype(F32)
    vf[...] = v_ref[0].astype(F32)

    for _, dil in DIL_CONFIGS:
        per_res = nblk // dil
        length = seq // dil
        dist = deltaf * float(dil)
        bias = jnp.concatenate([jnp.where(valid, nslope_e * dist, NEG_INF),
                                jnp.where(valid, nslope_o * dist, NEG_INF)], axis=0)
        if dil > 1:
            for res in range(dil):
                dst = slice(res * length, (res + 1) * length)
                kp[dst, :] = kf[pl.ds(res, length, stride=dil), :].astype(BF16)
                vp[dst, :] = vf[pl.ds(res, length, stride=dil), :].astype(BF16)

        for b in range(nblk):
            res, n = divmod(b, per_res)
            lo = (b if n == 0 else b - 1) * blk
            win = slice(lo, (b + 1) * blk)
            if dil == 1:
                rows = pl.ds(b * blk, blk)
                kw, vw = k_ref[0, win, :], v_ref[0, win, :]
            else:
                rows = pl.ds(dil * blk * n + res, blk, stride=dil)
                kw, vw = kp[win, :], vp[win, :]
            bb = bias[:, blk:] if n == 0 else bias
            qb = qf[rows, :]
            q_stack = jnp.concatenate(
                [jnp.where(even, qb, 0.0), jnp.where(even, 0.0, qb)], axis=0).astype(BF16)
            s = _dot_nt(q_stack, kw) + bb
            m = jnp.max(s, axis=-1, keepdims=True)
            p = jnp.exp(s - m)
            l = jnp.sum(p, axis=-1, keepdims=True)
            acc = jnp.dot(p.astype(BF16), vw, preferred_element_type=F32)
            m_b = jnp.where(even, m[:blk], m[blk:])
            l_b = jnp.where(even, l[:blk], l[blk:])
            a_b = jnp.where(even, acc[:blk], acc[blk:])
            if dil == 1:
                m_st[rows, :] = m_b
                l_st[rows, :] = l_b
                a_st[rows, :] = a_b
            else:
                m_old = m_st[rows, :]
                m_new = jnp.maximum(m_old, m_b)
                w_old = jnp.exp(m_old - m_new)
                w_blk = jnp.exp(m_b - m_new)
                m_st[rows, :] = m_new
                l_st[rows, :] = w_old * l_st[rows, :] + w_blk * l_b
                a_st[rows, :] = w_old * a_st[rows, :] + w_blk * a_b

    o_ref[0] = (a_st[...] * (1.0 / l_st[...])).astype(BF16)


def _dil_attention(h3, slope_pairs):
    b, seq, _ = h3.shape
    npairs = DIL_HEADS // 2
    f32_buf = pltpu.VMEM((seq, LANES), F32)
    bf16_buf = pltpu.VMEM((seq, LANES), BF16)
    return pl.pallas_call(
        functools.partial(_dil_kernel, seq=seq),
        grid=(b, npairs),
        in_specs=[
            pl.BlockSpec((1, seq, LANES), lambda i, p: (i, 0, COL_CQ + p)),
            pl.BlockSpec((1, seq, LANES), lambda i, p: (i, 0, COL_CK + p)),
            pl.BlockSpec((1, seq, LANES), lambda i, p: (i, 0, COL_CV + p)),
            pl.BlockSpec((1, 1, LANES), lambda i, p: (p, 0, 0)),
        ],
        out_specs=pl.BlockSpec((1, seq, LANES), lambda i, p: (i, 0, p)),
        out_shape=jax.ShapeDtypeStruct((b, seq, DIL_W), BF16),
        scratch_shapes=[f32_buf, f32_buf, f32_buf, bf16_buf, bf16_buf,
                        f32_buf, f32_buf, f32_buf],
        compiler_params=_params("arbitrary", "arbitrary"),
        name="dil_attn",
    )(h3, h3, h3, slope_pairs)


def _outproj_kernel(x_ref, fox_ref, diff_ref, dil_ref, w_ref, g_ref, b_ref, o_ref, *, alpha):
    mix = (jnp.dot(fox_ref[...], w_ref[0:FOX_W, :], preferred_element_type=F32)
           + jnp.dot(diff_ref[...], w_ref[FOX_W:FOX_W + DIFF_W, :], preferred_element_type=F32)
           + jnp.dot(dil_ref[...], w_ref[FOX_W + DIFF_W:, :], preferred_element_type=F32))
    y = alpha * x_ref[...] + mix
    o_ref[...] = _layer_norm(y, g_ref[...], b_ref[...])


def _outproj_ln(x2, fox, diff, dil, w_o, g, b, alpha):
    m = x2.shape[0]
    row = lambda w: pl.BlockSpec((ROW_TILE, w), lambda i: (i, 0))
    const = lambda r, w: pl.BlockSpec((r, w), lambda i: (0, 0))
    return pl.pallas_call(
        functools.partial(_outproj_kernel, alpha=alpha),
        grid=(m // ROW_TILE,),
        in_specs=[row(D_MODEL), row(FOX_W), row(DIFF_W), row(DIL_W),
                  const(MIX_W, D_MODEL), const(1, D_MODEL), const(1, D_MODEL)],
        out_specs=row(D_MODEL),
        out_shape=jax.ShapeDtypeStruct((m, D_MODEL), F32),
        compiler_params=_params("arbitrary"),
        name="outproj_ln",
    )(x2, fox, diff, dil, w_o, g, b)


def _memkv_kernel(m_ref, w_ref, o_ref):
    o_ref[0] = jnp.dot(m_ref[0].astype(BF16), w_ref[...],
                       preferred_element_type=F32).astype(BF16)


def _memkv(mem, w_kv):
    b, mt, _ = mem.shape
    return pl.pallas_call(
        _memkv_kernel,
        grid=(b,),
        in_specs=[pl.BlockSpec((1, mt, D_MODEL), lambda i: (i, 0, 0)),
                  pl.BlockSpec((D_MODEL, 2 * D_MODEL), lambda i: (0, 0))],
        out_specs=pl.BlockSpec((1, mt, 2 * D_MODEL), lambda i: (i, 0, 0)),
        out_shape=jax.ShapeDtypeStruct((b, mt, 2 * D_MODEL), BF16),
        compiler_params=_params("arbitrary"),
        name="memkv",
    )(mem, w_kv)


def _memattn_kernel(x_ref, kv_ref, wq_ref, wo_ref, g_ref, b_ref, o_ref, *, alpha):
    x = x_ref[0]
    q = jnp.dot(x.astype(BF16), wq_ref[...], preferred_element_type=F32)
    q = (q * (MEM_HEAD_DIM ** -0.5)).astype(BF16)
    y = alpha * x
    for h in range(MEM_HEADS):
        cols = slice(MEM_HEAD_DIM * h, MEM_HEAD_DIM * (h + 1))
        vcols = slice(D_MODEL + MEM_HEAD_DIM * h, D_MODEL + MEM_HEAD_DIM * (h + 1))
        s = _dot_nt(q[:, cols], kv_ref[0, :, cols])
        m = jnp.max(s, axis=-1, keepdims=True)
        p = jnp.exp(s - m)
        l = jnp.sum(p, axis=-1, keepdims=True)
        o = jnp.dot(p.astype(BF16), kv_ref[0, :, vcols], preferred_element_type=F32) * (1.0 / l)
        y = y + jnp.dot(o.astype(BF16), wo_ref[cols, :], preferred_element_type=F32)
    o_ref[0] = _layer_norm(y, g_ref[...], b_ref[...])


def _memattn_ln(x3, kv, w_q, w_o, g, b, alpha):
    bsz, seq, _ = x3.shape
    mt = kv.shape[1]
    const = lambda r, w: pl.BlockSpec((r, w), lambda i, j: (0, 0))
    return pl.pallas_call(
        functools.partial(_memattn_kernel, alpha=alpha),
        grid=(bsz, seq // ROW_TILE),
        in_specs=[
            pl.BlockSpec((1, ROW_TILE, D_MODEL), lambda i, j: (i, j, 0)),
            pl.BlockSpec((1, mt, 2 * D_MODEL), lambda i, j: (i, 0, 0)),
            const(D_MODEL, D_MODEL), const(D_MODEL, D_MODEL),
            const(1, D_MODEL), const(1, D_MODEL),
        ],
        out_specs=pl.BlockSpec((1, ROW_TILE, D_MODEL), lambda i, j: (i, j, 0)),
        out_shape=jax.ShapeDtypeStruct((bsz, seq, D_MODEL), F32),
        compiler_params=_params("arbitrary", "arbitrary"),
        name="memattn_ln",
    )(x3, kv, w_q, w_o, g, b)


def _ffn_kernel(x_ref, wup_ref, cp_ref, wdn_ref, g_ref, b_ref, o_ref, carry_ref, acc_ref,
                *, alpha):
    tm = ROW_TILE
    width = 2 * FF_CHUNK

    @pl.when(pl.program_id(1) == 0)
    def _():
        carry_ref[...] = jnp.zeros_like(carry_ref)

    x = x_ref[0]
    xb = x.astype(BF16)
    row = lax.broadcasted_iota(jnp.int32, (tm, width), 0)
    acc_ref[...] = alpha * x

    def chunk(c, carry):
        u = jnp.dot(xb, wup_ref[c], preferred_element_type=F32)
        cp = cp_ref[c]
        prev = carry_ref[c]
        carry_ref[c] = u[tm - 8:tm, :]
        u1 = jnp.where(row == 0, prev[7:8], pltpu.roll(u, 1, 0))
        u2 = jnp.where(row == 0, prev[6:7],
                       jnp.where(row == 1, prev[7:8], pltpu.roll(u, 2, 0)))
        y = cp[0:1] * u2 + cp[1:2] * u1 + cp[2:3] * u + cp[3:4]
        h = (jax.nn.gelu(y[:, :FF_CHUNK]) * y[:, FF_CHUNK:]).astype(BF16)
        acc_ref[...] += jnp.dot(h, wdn_ref[c], preferred_element_type=F32)
        return carry

    lax.fori_loop(0, N_FF_CHUNKS, chunk, 0)
    o_ref[0] = _layer_norm(acc_ref[...], g_ref[...], b_ref[...])


def _ffn_ln(x3, w_up, cp, w_dn, g, b, alpha):
    bsz, seq, _ = x3.shape
    width = 2 * FF_CHUNK
    const2 = lambda r, w: pl.BlockSpec((r, w), lambda i, j: (0, 0))
    const3 = lambda a, r, w: pl.BlockSpec((a, r, w), lambda i, j: (0, 0, 0))
    return pl.pallas_call(
        functools.partial(_ffn_kernel, alpha=alpha),
        grid=(bsz, seq // ROW_TILE),
        in_specs=[
            pl.BlockSpec((1, ROW_TILE, D_MODEL), lambda i, j: (i, j, 0)),
            const3(N_FF_CHUNKS, D_MODEL, width),
            const3(N_FF_CHUNKS, 8, width),
            const3(N_FF_CHUNKS, FF_CHUNK, D_MODEL),
            const2(1, D_MODEL), const2(1, D_MODEL),
        ],
        out_specs=pl.BlockSpec((1, ROW_TILE, D_MODEL), lambda i, j: (i, j, 0)),
        out_shape=jax.ShapeDtypeStruct((bsz, seq, D_MODEL), F32),
        scratch_shapes=[pltpu.VMEM((N_FF_CHUNKS, 8, width), F32),
                        pltpu.VMEM((ROW_TILE, D_MODEL), F32)],
        compiler_params=_params("arbitrary", "arbitrary"),
        name="ffn_ln",
    )(x3, w_up, cp, w_dn, g, b)


def _pad_lanes(v):
    return jnp.pad(v.astype(F32), (0, LANES - v.shape[0])).reshape(1, LANES)


def kernel(x, mem, w_in, b_f, lambda_q1, lambda_k1, lambda_q2, lambda_k2, diff_norm_g, w_o,
           ln1_g, ln1_b, w_mq, w_mkv, w_mo, ln2_g, ln2_b, w_up, conv_w, conv_b, w_down,
           ln3_g, ln3_b):
    bsz, seq, _ = x.shape
    depth = w_in.shape[0]
    alpha = (2 * depth) ** 0.25
    diff_slopes, dil_slopes = _alibi_slopes()
    slope_pairs = jnp.asarray(
        np.repeat(dil_slopes.reshape(DIL_HEADS // 2, 2), HEAD_DIM, axis=1)
        .reshape(DIL_HEADS // 2, 1, LANES))
    ff_start = 3 * FOX_W
    row2 = lambda v: v.astype(F32).reshape(1, D_MODEL)

    x2 = x.reshape(bsz * seq, D_MODEL)
    for l in range(depth):
        lam_init = 0.8 - 0.6 * math.exp(-0.3 * l)
        w_main = jnp.concatenate(
            [w_in[l][:, :ff_start], w_in[l][:, ff_start + FOX_HEADS:]], axis=1).astype(BF16)
        w_ff = jnp.pad(w_in[l][:, ff_start:ff_start + FOX_HEADS],
                       ((0, 0), (0, LANES - FOX_HEADS))).astype(BF16)
        h2, f2 = _inproj(x2, w_main, w_ff)
        h3 = h2.reshape(bsz, seq, MAIN_COLS)
        ccol, crow = _fox_prep(f2.reshape(bsz, seq, LANES), _pad_lanes(b_f[l]))
        fox = _fox_attention(h3, ccol, crow)
        lamp = jnp.zeros((8, LANES), F32)
        for i, v in enumerate((lambda_q1, lambda_k1, lambda_q2, lambda_k2)):
            lamp = lamp.at[i, :DIFF_QK_DIM].set(v[l].astype(F32))
        gain_row = jnp.tile(diff_norm_g[l].astype(F32), 2).reshape(1, LANES)
        diff = _diff_attention(h3, lamp, gain_row, diff_slopes, lam_init)
        dil = _dil_attention(h3, slope_pairs)
        x2 = _outproj_ln(x2, fox.reshape(bsz * seq, FOX_W), diff.reshape(bsz * seq, DIFF_W),
                         dil.reshape(bsz * seq, DIL_W), w_o[l].astype(BF16),
                         row2(ln1_g[l]), row2(ln1_b[l]), alpha)
        kv = _memkv(mem, w_mkv[l].astype(BF16))
        x3 = _memattn_ln(x2.reshape(bsz, seq, D_MODEL), kv, w_mq[l].astype(BF16),
                         w_mo[l].astype(BF16), row2(ln2_g[l]), row2(ln2_b[l]), alpha)
        wu = w_up[l].reshape(D_MODEL, 2, N_FF_CHUNKS, FF_CHUNK).transpose(2, 0, 1, 3)
        wu = wu.reshape(N_FF_CHUNKS, D_MODEL, 2 * FF_CHUNK).astype(BF16)
        taps = jnp.concatenate([conv_w[l][:, 0, :], conv_b[l][None, :]], axis=0).astype(F32)
        taps = taps.reshape(4, 2, N_FF_CHUNKS, FF_CHUNK).transpose(2, 0, 1, 3)
        cp = jnp.pad(taps.reshape(N_FF_CHUNKS, 4, 2 * FF_CHUNK), ((0, 0), (0, 4), (0, 0)))
        wd = w_down[l].reshape(N_FF_CHUNKS, FF_CHUNK, D_MODEL).astype(BF16)
        x3 = _ffn_ln(x3, wu, cp, wd, row2(ln3_g[l]), row2(ln3_b[l]), alpha)
        x2 = x3.reshape(bsz * seq, D_MODEL)
    return x2.reshape(bsz, seq, D_MODEL)
```

```python
import functools
import math

import numpy as np
import jax
import jax.numpy as jnp
from jax import lax
from jax.experimental import pallas as pl
from jax.experimental.pallas import tpu as pltpu

F32 = jnp.float32
BF16 = jnp.bfloat16

D_MODEL = 1024
HEAD_DIM = 64
FOX_HEADS = 4
DIFF_HEADS = 4
DIL_HEADS = 8
FOX_W = FOX_HEADS * HEAD_DIM
DIFF_W = DIFF_HEADS * HEAD_DIM
DIL_W = DIL_HEADS * HEAD_DIM
MIX_W = FOX_W + DIFF_W + DIL_W
DIFF_QK_DIM = HEAD_DIM // 2
DIL_CONFIGS = ((128, 1), (512, 4), (2048, 16))
DIL_BLOCK = 128
MEM_HEADS = 4
MEM_HEAD_DIM = D_MODEL // MEM_HEADS
D_FF = 2816
LN_EPS = 1e-5
NEG_INF = -1e30

LANES = 128
VMEM_LIMIT = 56 * 1024 * 1024

ROW_TILE = 512
IN_CHUNK = 512
ATT_Q = 512
ATT_K = 256
FF_CHUNK = 256
N_FF_CHUNKS = D_FF // FF_CHUNK
FF_ROWS = 64
PREP_BLOCK = 256

COL_FQ, COL_FK, COL_FV, COL_DQ, COL_DK, COL_DV = 0, 1, 2, 3, 4, 5
COL_CQ, COL_CK, COL_CV = 12, 16, 20
MAIN_COLS = 3 * MIX_W


def _alibi_slopes():
    n = DIFF_HEADS + DIL_HEADS
    slopes = 2.0 ** (-8.0 * np.arange(1, n + 1) / n)
    stride = n // DIFF_HEADS
    diff_sel = np.arange(0, n, stride)[:DIFF_HEADS]
    dil_sel = np.setdiff1d(np.arange(n), diff_sel)
    return (np.asarray(slopes[diff_sel], np.float32), np.asarray(slopes[dil_sel], np.float32))


def _params(*sem):
    return pltpu.CompilerParams(dimension_semantics=sem, vmem_limit_bytes=VMEM_LIMIT)


def _layer_norm(y, g, b):
    mu = jnp.mean(y, axis=-1, keepdims=True)
    d = y - mu
    var = jnp.mean(d * d, axis=-1, keepdims=True)
    return d * lax.rsqrt(var + LN_EPS) * g + b


def _dot_nt(a, b):
    return lax.dot_general(a, b, (((1,), (1,)), ((), ())), preferred_element_type=F32)


def _inproj_kernel(x_ref, w_ref, wf_ref, h_ref, f_ref):
    xb = x_ref[...].astype(BF16)
    for n in range(0, MAIN_COLS, IN_CHUNK):
        h_ref[:, n:n + IN_CHUNK] = jnp.dot(
            xb, w_ref[:, n:n + IN_CHUNK], preferred_element_type=F32).astype(BF16)
    f_ref[...] = jnp.dot(xb, wf_ref[...], preferred_element_type=F32)


def _inproj(x2, w_main, w_ff):
    m = x2.shape[0]
    return pl.pallas_call(
        _inproj_kernel,
        grid=(m // ROW_TILE,),
        in_specs=[
            pl.BlockSpec((ROW_TILE, D_MODEL), lambda i: (i, 0)),
            pl.BlockSpec((D_MODEL, MAIN_COLS), lambda i: (0, 0)),
            pl.BlockSpec((D_MODEL, LANES), lambda i: (0, 0)),
        ],
        out_specs=[
            pl.BlockSpec((ROW_TILE, MAIN_COLS), lambda i: (i, 0)),
            pl.BlockSpec((ROW_TILE, LANES), lambda i: (i, 0)),
        ],
        out_shape=[
            jax.ShapeDtypeStruct((m, MAIN_COLS), BF16),
            jax.ShapeDtypeStruct((m, LANES), F32),
        ],
        compiler_params=_params("arbitrary"),
        name="inproj",
    )(x2, w_main, w_ff)


def _split3(x):
    hi = x.astype(BF16).astype(F32)
    r1 = x - hi
    mid = r1.astype(BF16).astype(F32)
    lo = (r1 - mid).astype(BF16).astype(F32)
    return hi, mid, lo


def _fox_prep_kernel(f_ref, bf_ref, qa_ref, ka_ref, *, seq):
    r = lax.broadcasted_iota(jnp.int32, (PREP_BLOCK, PREP_BLOCK), 0)
    c = lax.broadcasted_iota(jnp.int32, (PREP_BLOCK, PREP_BLOCK), 1)
    tri = jnp.where(c <= r, 1.0, 0.0).astype(BF16)
    lane = lax.broadcasted_iota(jnp.int32, (1, LANES), 1)

    def pick(pieces, base):
        return jnp.where(lane == base, pieces[0], jnp.where(lane == base + 1, pieces[1], pieces[2]))

    offset = jnp.zeros((1, LANES), F32)
    for blk in range(seq // PREP_BLOCK):
        rows = slice(blk * PREP_BLOCK, (blk + 1) * PREP_BLOCK)
        x = f_ref[0, rows, :] + bf_ref[...]
        logf = jnp.minimum(x, 0.0) - jnp.log1p(jnp.exp(-jnp.abs(x)))
        cs = sum(jnp.dot(tri, piece.astype(BF16), preferred_element_type=F32)
                 for piece in _split3(logf))
        cblk = cs + offset
        offset = cblk[PREP_BLOCK - 1:PREP_BLOCK, :]
        for pp in range(FOX_HEADS // 2):
            cols = slice(LANES * pp, LANES * (pp + 1))
            ce = _split3(jnp.broadcast_to(cblk[:, 2 * pp:2 * pp + 1], (PREP_BLOCK, LANES)))
            co = _split3(jnp.broadcast_to(cblk[:, 2 * pp + 1:2 * pp + 2], (PREP_BLOCK, LANES)))
            ka = jnp.where(lane < 3, 1.0,
                           jnp.where(lane < 6, -pick(ce, 3),
                                     jnp.where(lane < 9, 1.0,
                                               jnp.where(lane < 12, -pick(co, 9), 0.0))))
            qa_e = jnp.where(lane < 3, pick(ce, 0), jnp.where(lane < 6, 1.0, 0.0))
            qa_o = jnp.where(lane < 6, 0.0,
                             jnp.where(lane < 9, pick(co, 6), jnp.where(lane < 12, 1.0, 0.0)))
            ka_ref[0, rows, cols] = ka.astype(BF16)
            qa_ref[0, 0, rows, cols] = qa_e.astype(BF16)
            qa_ref[0, 1, rows, cols] = qa_o.astype(BF16)


def _fox_prep(f3, bf_row):
    b, seq, _ = f3.shape
    width = LANES * (FOX_HEADS // 2)
    return pl.pallas_call(
        functools.partial(_fox_prep_kernel, seq=seq),
        grid=(b,),
        in_specs=[
            pl.BlockSpec((1, seq, LANES), lambda i: (i, 0, 0)),
            pl.BlockSpec((1, LANES), lambda i: (0, 0)),
        ],
        out_specs=[
            pl.BlockSpec((1, 2, seq, width), lambda i: (i, 0, 0, 0)),
            pl.BlockSpec((1, seq, width), lambda i: (i, 0, 0)),
        ],
        out_shape=[
            jax.ShapeDtypeStruct((b, 2, seq, width), BF16),
            jax.ShapeDtypeStruct((b, seq, width), BF16),
        ],
        compiler_params=_params("arbitrary"),
        name="fox_prep",
    )(f3, bf_row)


def _causal_two_phase(q_aug, qi, key_tile, value_tile, log2_scale, s_sc, mp_sc, acc_sc):
    tq, tk = ATT_Q, ATT_K
    per_step = 2
    n_diag = tq // tk
    assert n_diag == per_step
    n_pairs = len(q_aug)
    rows = q_aug[0].shape[0]
    r = lax.broadcasted_iota(jnp.int32, (rows, tk), 0) & (tq - 1)
    c = lax.broadcasted_iota(jnp.int32, (rows, tk), 1)
    ones = jnp.ones((tk, LANES), BF16)
    mp_sc[...] = jnp.full(mp_sc.shape, NEG_INF, F32)

    def phase1(step, diagonal):
        for pp in range(n_pairs):
            part = mp_sc[pp]
            for i in range(per_step):
                k0 = pl.multiple_of((step * per_step + i) * tk, tk)
                s = _dot_nt(q_aug[pp], key_tile(pp, k0)) * log2_scale
                if diagonal:
                    s = jnp.where(c + i * tk <= r, s, NEG_INF)
                s_sc[pp, :, pl.ds(k0, tk)] = s
                part = jnp.maximum(part, jnp.maximum(s[:, :LANES], s[:, LANES:]))
            mp_sc[pp] = part

    def phase1_body(step, carry):
        phase1(step, False)
        return carry

    lax.fori_loop(0, qi, phase1_body, 0)
    phase1(qi, True)

    acc_sc[...] = jnp.zeros(acc_sc.shape, F32)
    row_max = [jnp.max(mp_sc[pp], axis=-1, keepdims=True) for pp in range(n_pairs)]

    def phase2_body(step, carry):
        for pp in range(n_pairs):
            pv = None
            for i in range(per_step):
                k0 = pl.multiple_of((step * per_step + i) * tk, tk)
                p = jnp.exp2(s_sc[pp, :, pl.ds(k0, tk)] - row_max[pp])
                d = jnp.dot(p.astype(BF16), jnp.concatenate([value_tile(pp, k0), ones], axis=1),
                            preferred_element_type=F32)
                pv = d if pv is None else pv + d
            acc_sc[pp] += pv
        return carry

    lax.fori_loop(0, qi + 1, phase2_body, 0)
    return [(acc_sc[pp, :, :LANES], acc_sc[pp, :, LANES:LANES + 1]) for pp in range(n_pairs)]


def _attn_scratch(n_pairs, rows, seq):
    return [pltpu.VMEM((n_pairs, rows, seq), F32), pltpu.VMEM((n_pairs, rows, LANES), F32),
            pltpu.VMEM((n_pairs, rows, 2 * LANES), F32)]


def _fox_kernel(q_ref, k_ref, v_ref, qa_ref, ka_ref, o_ref, s_sc, mp_sc, acc_sc):
    t = ATT_Q
    n_pairs = FOX_HEADS // 2
    qi = pl.program_id(1)
    lane = lax.broadcasted_iota(jnp.int32, (1, LANES), 1)
    even = lane < HEAD_DIM
    cols = [slice(LANES * pp, LANES * (pp + 1)) for pp in range(n_pairs)]
    q_aug = []
    for pp in range(n_pairs):
        qp = q_ref[0, :, cols[pp]].astype(F32) * (HEAD_DIM ** -0.5)
        heads = jnp.concatenate(
            [jnp.where(even, qp, 0.0), jnp.where(even, 0.0, qp)], axis=0).astype(BF16)
        bias = jnp.concatenate([qa_ref[0, 0, :, cols[pp]], qa_ref[0, 1, :, cols[pp]]], axis=0)
        q_aug.append(jnp.concatenate([heads, bias], axis=1))

    def key_tile(pp, k0):
        return jnp.concatenate([k_ref[0, pl.ds(k0, ATT_K), cols[pp]],
                                ka_ref[0, pl.ds(k0, ATT_K), cols[pp]]], axis=1)

    def value_tile(pp, k0):
        return v_ref[0, pl.ds(k0, ATT_K), cols[pp]]

    res = _causal_two_phase(q_aug, qi, key_tile, value_tile, math.log2(math.e),
                            s_sc, mp_sc, acc_sc)
    for pp in range(n_pairs):
        acc, l = res[pp]
        o = acc * (1.0 / l)
        o_ref[0, :, cols[pp]] = jnp.where(even, o[:t], o[t:]).astype(BF16)


def _fox_attention(h3, qa, ka):
    b, seq, _ = h3.shape
    t = ATT_Q
    return pl.pallas_call(
        _fox_kernel,
        grid=(b, seq // t),
        in_specs=[
            pl.BlockSpec((1, t, FOX_W), lambda i, j: (i, j, COL_FQ)),
            pl.BlockSpec((1, seq, FOX_W), lambda i, j: (i, 0, COL_FK)),
            pl.BlockSpec((1, seq, FOX_W), lambda i, j: (i, 0, COL_FV)),
            pl.BlockSpec((1, 2, t, FOX_W), lambda i, j: (i, 0, j, 0)),
            pl.BlockSpec((1, seq, FOX_W), lambda i, j: (i, 0, 0)),
        ],
        out_specs=pl.BlockSpec((1, t, FOX_W), lambda i, j: (i, j, 0)),
        out_shape=jax.ShapeDtypeStruct((b, seq, FOX_W), BF16),
        scratch_shapes=_attn_scratch(FOX_HEADS // 2, 2 * t, seq),
        compiler_params=_params("arbitrary", "arbitrary"),
        name="fox_attn",
    )(h3, h3, h3, qa, ka)


def _np_split3(x):
    hi = x.astype(BF16)
    r1 = x - hi.astype(np.float64)
    mid = r1.astype(BF16)
    lo = (r1 - mid.astype(np.float64)).astype(BF16)
    return hi, mid, lo


def _diff_alibi_columns(seq, slopes):
    scale = DIFF_QK_DIM ** -0.5
    pos = np.arange(seq)
    ka = np.zeros((seq, LANES), np.float32)
    ka[:, 0:3] = 1.0
    ka[:, 3:6] = (8 * (pos // 8))[:, None]
    ka[:, 6:9] = (pos % 8)[:, None]
    qa = np.zeros((DIFF_HEADS // 2, 4, seq, LANES), np.float32)
    for pp in range(DIFF_HEADS // 2):
        for mi in range(4):
            sl = float(slopes[2 * pp + mi // 2]) / scale
            a = _np_split3(-sl * pos.astype(np.float64))
            b = _np_split3(np.full(seq, sl, np.float64))
            for i in range(3):
                qa[pp, mi, :, i] = a[i].astype(np.float32)
                qa[pp, mi, :, 3 + i] = b[i].astype(np.float32)
                qa[pp, mi, :, 6 + i] = b[i].astype(np.float32)
    return jnp.asarray(qa, BF16), jnp.asarray(ka, BF16)


def _diff_kernel(q_ref, k_ref, v_ref, qa_ref, ka_ref, lamp_ref, g_ref, o_ref,
                 s_sc, mp_sc, acc_sc, *, lam_init):
    t = ATT_Q
    qi = pl.program_id(2)
    lane = lax.broadcasted_iota(jnp.int32, (1, LANES), 1)
    lamp = lamp_ref[...]
    a1 = jnp.sum(lamp[0:1] * lamp[1:2], axis=-1, keepdims=True)
    a2 = jnp.sum(lamp[2:3] * lamp[3:4], axis=-1, keepdims=True)
    lam = jnp.exp(a1) - jnp.exp(a2) + lam_init
    gain = g_ref[...]
    qp = q_ref[0].astype(F32)
    groups = []
    for mi in range(4):
        sel = (lane >= DIFF_QK_DIM * mi) & (lane < DIFF_QK_DIM * (mi + 1))
        groups.append(jnp.where(sel, qp, 0.0))
    maps = jnp.concatenate(groups, axis=0).astype(BF16)
    q_aug = [jnp.concatenate([maps, qa_ref[0].reshape(4 * t, LANES)], axis=1)]

    def key_tile(pp, k0):
        return jnp.concatenate([k_ref[0, pl.ds(k0, ATT_K), :], ka_ref[pl.ds(k0, ATT_K), :]],
                               axis=1)

    def value_tile(pp, k0):
        return v_ref[0, pl.ds(k0, ATT_K), :]

    (acc, l), = _causal_two_phase(q_aug, qi, key_tile, value_tile,
                                  DIFF_QK_DIM ** -0.5 * math.log2(math.e), s_sc, mp_sc, acc_sc)
    o = acc * (1.0 / l)
    outs = []
    for hh in range(2):
        oh = o[(2 * hh) * t:(2 * hh + 1) * t] - lam * o[(2 * hh + 1) * t:(2 * hh + 2) * t]
        half = (lane >= HEAD_DIM * hh) & (lane < HEAD_DIM * (hh + 1))
        ms = jnp.sum(jnp.where(half, oh * oh, 0.0), axis=-1, keepdims=True) * (1.0 / HEAD_DIM)
        outs.append(oh * lax.rsqrt(ms + LN_EPS) * gain * (1.0 - lam_init))
    o_ref[0] = jnp.where(lane < HEAD_DIM, outs[0], outs[1]).astype(BF16)


def _diff_attention(h3, lamp, gain_row, slopes, lam_init):
    b, seq, _ = h3.shape
    t = ATT_Q
    n_pairs = DIFF_HEADS // 2
    qa, ka = _diff_alibi_columns(seq, slopes)
    q0, k0, v0 = (c * (DIFF_W // LANES) for c in (COL_DQ, COL_DK, COL_DV))
    return pl.pallas_call(
        functools.partial(_diff_kernel, lam_init=lam_init),
        grid=(b, n_pairs, seq // t),
        in_specs=[
            pl.BlockSpec((1, t, LANES), lambda i, p, j: (i, j, q0 + p)),
            pl.BlockSpec((1, seq, LANES), lambda i, p, j: (i, 0, k0 + p)),
            pl.BlockSpec((1, seq, LANES), lambda i, p, j: (i, 0, v0 + p)),
            pl.BlockSpec((1, 4, t, LANES), lambda i, p, j: (p, 0, j, 0)),
            pl.BlockSpec((seq, LANES), lambda i, p, j: (0, 0)),
            pl.BlockSpec((8, LANES), lambda i, p, j: (0, 0)),
            pl.BlockSpec((1, LANES), lambda i, p, j: (0, 0)),
        ],
        out_specs=pl.BlockSpec((1, t, LANES), lambda i, p, j: (i, j, p)),
        out_shape=jax.ShapeDtypeStruct((b, seq, DIFF_W), BF16),
        scratch_shapes=_attn_scratch(1, 4 * t, seq),
        compiler_params=_params("arbitrary", "arbitrary", "arbitrary"),
        name="diff_attn",
    )(h3, h3, h3, qa, ka, lamp, gain_row)


def _dil_kernel(q_ref, k_ref, v_ref, slope_ref, o_ref,
                qf, kf, vf, kp, vp, m_st, l_st, a_st, *, seq):
    blk = DIL_BLOCK
    nblk = seq // blk
    lane = lax.broadcasted_iota(jnp.int32, (1, LANES), 1)
    even = lane < HEAD_DIM
    slope = slope_ref[0]
    nslope_e = -slope[:, 0:1]
    nslope_o = -slope[:, HEAD_DIM:HEAD_DIM + 1]
    qi = lax.broadcasted_iota(jnp.int32, (blk, 2 * blk), 0)
    ki = lax.broadcasted_iota(jnp.int32, (blk, 2 * blk), 1) - blk
    delta = qi - ki
    valid = (delta >= 0) & (delta <= blk)
    deltaf = delta.astype(F32)

    qf[...] = q_ref[0].astype(F32) * (HEAD_DIM ** -0.5)
    kf[...] = k_ref[0].astype(F32)
    vf[...] = v_ref[0].astype(F32)

    def block(rows, kw, vw, bias, merge):
        qb = qf[rows, :]
        q_stack = jnp.concatenate(
            [jnp.where(even, qb, 0.0), jnp.where(even, 0.0, qb)], axis=0).astype(BF16)
        s = _dot_nt(q_stack, kw) + bias
        m = jnp.max(s, axis=-1, keepdims=True)
        p = jnp.exp(s - m)
        l = jnp.sum(p, axis=-1, keepdims=True)
        acc = jnp.dot(p.astype(BF16), vw, preferred_element_type=F32)
        m_b = jnp.where(even, m[:blk], m[blk:])
        l_b = jnp.where(even, l[:blk], l[blk:])
        a_b = jnp.where(even, acc[:blk], acc[blk:])
        if merge:
            m_old = m_st[rows, :]
            m_new = jnp.maximum(m_old, m_b)
            w_old = jnp.exp(m_old - m_new)
            w_blk = jnp.exp(m_b - m_new)
            m_st[rows, :] = m_new
            l_st[rows, :] = w_old * l_st[rows, :] + w_blk * l_b
            a_st[rows, :] = w_old * a_st[rows, :] + w_blk * a_b
        else:
            m_st[rows, :] = m_b
            l_st[rows, :] = l_b
            a_st[rows, :] = a_b

    for _, dil in DIL_CONFIGS:
        per_res = nblk // dil
        length = seq // dil
        dist = deltaf * float(dil)
        bias = jnp.concatenate([jnp.where(valid, nslope_e * dist, NEG_INF),
                                jnp.where(valid, nslope_o * dist, NEG_INF)], axis=0)
        bias_first = bias[:, blk:]

        for b in range(nblk):
            res, n = divmod(b, per_res)
            win = slice((b - 1 if n else b) * blk, (b + 1) * blk)
            if dil == 1:
                block(pl.ds(b * blk, blk), k_ref[0, win, :], v_ref[0, win, :],
                      bias if n else bias_first, False)
                continue
            if n == 0:
                dst = slice(res * length, (res + 1) * length)
                kp[dst, :] = kf[pl.ds(res, length, stride=dil), :].astype(BF16)
                vp[dst, :] = vf[pl.ds(res, length, stride=dil), :].astype(BF16)
            block(pl.ds(dil * blk * n + res, blk, stride=dil), kp[win, :], vp[win, :],
                  bias if n else bias_first, True)

    o_ref[0] = (a_st[...] * (1.0 / l_st[...])).astype(BF16)


def _dil_attention(h3, slope_pairs):
    b, seq, _ = h3.shape
    npairs = DIL_HEADS // 2
    f32_buf = pltpu.VMEM((seq, LANES), F32)
    bf16_buf = pltpu.VMEM((seq, LANES), BF16)
    return pl.pallas_call(
        functools.partial(_dil_kernel, seq=seq),
        grid=(b, npairs),
        in_specs=[
            pl.BlockSpec((1, seq, LANES), lambda i, p: (i, 0, COL_CQ + p)),
            pl.BlockSpec((1, seq, LANES), lambda i, p: (i, 0, COL_CK + p)),
            pl.BlockSpec((1, seq, LANES), lambda i, p: (i, 0, COL_CV + p)),
            pl.BlockSpec((1, 1, LANES), lambda i, p: (p, 0, 0)),
        ],
        out_specs=pl.BlockSpec((1, seq, LANES), lambda i, p: (i, 0, p)),
        out_shape=jax.ShapeDtypeStruct((b, seq, DIL_W), BF16),
        scratch_shapes=[f32_buf, f32_buf, f32_buf, bf16_buf, bf16_buf,
                        f32_buf, f32_buf, f32_buf],
        compiler_params=_params("arbitrary", "arbitrary"),
        name="dil_attn",
    )(h3, h3, h3, slope_pairs)


def _outproj_kernel(x_ref, fox_ref, diff_ref, dil_ref, w_ref, g_ref, b_ref, o_ref, *, alpha):
    mix = (jnp.dot(fox_ref[...], w_ref[0:FOX_W, :], preferred_element_type=F32)
           + jnp.dot(diff_ref[...], w_ref[FOX_W:FOX_W + DIFF_W, :], preferred_element_type=F32)
           + jnp.dot(dil_ref[...], w_ref[FOX_W + DIFF_W:, :], preferred_element_type=F32))
    y = alpha * x_ref[...] + mix
    o_ref[...] = _layer_norm(y, g_ref[...], b_ref[...])


def _outproj_ln(x2, fox, diff, dil, w_o, g, b, alpha):
    m = x2.shape[0]
    row = lambda w: pl.BlockSpec((ROW_TILE, w), lambda i: (i, 0))
    const = lambda r, w: pl.BlockSpec((r, w), lambda i: (0, 0))
    return pl.pallas_call(
        functools.partial(_outproj_kernel, alpha=alpha),
        grid=(m // ROW_TILE,),
        in_specs=[row(D_MODEL), row(FOX_W), row(DIFF_W), row(DIL_W),
                  const(MIX_W, D_MODEL), const(1, D_MODEL), const(1, D_MODEL)],
        out_specs=row(D_MODEL),
        out_shape=jax.ShapeDtypeStruct((m, D_MODEL), F32),
        compiler_params=_params("arbitrary"),
        name="outproj_ln",
    )(x2, fox, diff, dil, w_o, g, b)


def _memkv_kernel(m_ref, w_ref, o_ref):
    o_ref[0] = jnp.dot(m_ref[0].astype(BF16), w_ref[...],
                       preferred_element_type=F32).astype(BF16)


def _memkv(mem, w_kv):
    b, mt, _ = mem.shape
    return pl.pallas_call(
        _memkv_kernel,
        grid=(b,),
        in_specs=[pl.BlockSpec((1, mt, D_MODEL), lambda i: (i, 0, 0)),
                  pl.BlockSpec((D_MODEL, 2 * D_MODEL), lambda i: (0, 0))],
        out_specs=pl.BlockSpec((1, mt, 2 * D_MODEL), lambda i: (i, 0, 0)),
        out_shape=jax.ShapeDtypeStruct((b, mt, 2 * D_MODEL), BF16),
        compiler_params=_params("arbitrary"),
        name="memkv",
    )(mem, w_kv)


def _memattn_kernel(x_ref, kv_ref, wq_ref, wo_ref, g_ref, b_ref, o_ref, *, alpha):
    x = x_ref[0]
    q = jnp.dot(x.astype(BF16), wq_ref[...], preferred_element_type=F32)
    q = (q * (MEM_HEAD_DIM ** -0.5)).astype(BF16)
    y = alpha * x
    for h in range(MEM_HEADS):
        cols = slice(MEM_HEAD_DIM * h, MEM_HEAD_DIM * (h + 1))
        vcols = slice(D_MODEL + MEM_HEAD_DIM * h, D_MODEL + MEM_HEAD_DIM * (h + 1))
        s = _dot_nt(q[:, cols], kv_ref[0, :, cols])
        m = jnp.max(s, axis=-1, keepdims=True)
        p = jnp.exp(s - m)
        l = jnp.sum(p, axis=-1, keepdims=True)
        o = jnp.dot(p.astype(BF16), kv_ref[0, :, vcols], preferred_element_type=F32) * (1.0 / l)
        y = y + jnp.dot(o.astype(BF16), wo_ref[cols, :], preferred_element_type=F32)
    o_ref[0] = _layer_norm(y, g_ref[...], b_ref[...])


def _memattn_ln(x3, kv, w_q, w_o, g, b, alpha):
    bsz, seq, _ = x3.shape
    mt = kv.shape[1]
    const = lambda r, w: pl.BlockSpec((r, w), lambda i, j: (0, 0))
    return pl.pallas_call(
        functools.partial(_memattn_kernel, alpha=alpha),
        grid=(bsz, seq // ROW_TILE),
        in_specs=[
            pl.BlockSpec((1, ROW_TILE, D_MODEL), lambda i, j: (i, j, 0)),
            pl.BlockSpec((1, mt, 2 * D_MODEL), lambda i, j: (i, 0, 0)),
            const(D_MODEL, D_MODEL), const(D_MODEL, D_MODEL),
            const(1, D_MODEL), const(1, D_MODEL),
        ],
        out_specs=pl.BlockSpec((1, ROW_TILE, D_MODEL), lambda i, j: (i, j, 0)),
        out_shape=jax.ShapeDtypeStruct((bsz, seq, D_MODEL), F32),
        compiler_params=_params("arbitrary", "arbitrary"),
        name="memattn_ln",
    )(x3, kv, w_q, w_o, g, b)


def _ffn_kernel(x_ref, wup_ref, taps_ref, wdn_ref, g_ref, b_ref, o_ref,
                carry_ref, u_ref, h_ref, acc_ref, *, alpha):
    tm = ROW_TILE
    n_slabs = 2 * FF_CHUNK // LANES

    @pl.when(pl.program_id(1) == 0)
    def _():
        carry_ref[...] = jnp.zeros_like(carry_ref)

    x = x_ref[0]
    xb = x.astype(BF16)
    acc_ref[...] = alpha * x

    def slab_cols(c, slab):
        part, half = divmod(slab, n_slabs // 2)
        lo = part * D_FF + c * FF_CHUNK + half * LANES
        return slice(lo, lo + LANES)

    def up_project(c):
        slot = c % 2
        for part in range(2):
            lo = part * D_FF + c * FF_CHUNK
            u = jnp.dot(xb, wup_ref[:, lo:lo + FF_CHUNK], preferred_element_type=F32)
            for half in range(n_slabs // 2):
                slab = part * (n_slabs // 2) + half
                piece = u[:, half * LANES:(half + 1) * LANES]
                u_ref[slot, slab, 0:8, :] = carry_ref[n_slabs * c + slab]
                u_ref[slot, slab, 8:8 + tm, :] = piece
                carry_ref[n_slabs * c + slab] = piece[tm - 8:tm, :]

    def conv_glu(c):
        slot = c % 2
        taps = [taps_ref[:, slab_cols(c, slab)] for slab in range(n_slabs)]
        for r in range(0, tm, FF_ROWS):
            ys = []
            for slab in range(n_slabs):
                t = taps[slab]
                ys.append(t[0:1] * u_ref[slot, slab, 6 + r:6 + r + FF_ROWS, :]
                          + t[1:2] * u_ref[slot, slab, 7 + r:7 + r + FF_ROWS, :]
                          + t[2:3] * u_ref[slot, slab, 8 + r:8 + r + FF_ROWS, :] + t[3:4])
            for half in range(n_slabs // 2):
                h = jax.nn.gelu(ys[half]) * ys[n_slabs // 2 + half]
                h_ref[slot, r:r + FF_ROWS, half * LANES:(half + 1) * LANES] = h.astype(BF16)

    def down_project(c):
        acc_ref[...] += jnp.dot(h_ref[c % 2], wdn_ref[c * FF_CHUNK:(c + 1) * FF_CHUNK, :],
                                preferred_element_type=F32)

    up_project(0)
    for c in range(N_FF_CHUNKS):
        if c + 1 < N_FF_CHUNKS:
            up_project(c + 1)
        conv_glu(c)
        down_project(c)
    o_ref[0] = _layer_norm(acc_ref[...], g_ref[...], b_ref[...])


def _ffn_ln(x3, w_up, taps, w_dn, g, b, alpha):
    bsz, seq, _ = x3.shape
    const = lambda r, w: pl.BlockSpec((r, w), lambda i, j: (0, 0))
    return pl.pallas_call(
        functools.partial(_ffn_kernel, alpha=alpha),
        grid=(bsz, seq // ROW_TILE),
        in_specs=[
            pl.BlockSpec((1, ROW_TILE, D_MODEL), lambda i, j: (i, j, 0)),
            const(D_MODEL, 2 * D_FF), const(8, 2 * D_FF), const(D_FF, D_MODEL),
            const(1, D_MODEL), const(1, D_MODEL),
        ],
        out_specs=pl.BlockSpec((1, ROW_TILE, D_MODEL), lambda i, j: (i, j, 0)),
        out_shape=jax.ShapeDtypeStruct((bsz, seq, D_MODEL), F32),
        scratch_shapes=[pltpu.VMEM((2 * D_FF // LANES, 8, LANES), F32),
                        pltpu.VMEM((2, 2 * FF_CHUNK // LANES, 8 + ROW_TILE, LANES), F32),
                        pltpu.VMEM((2, ROW_TILE, FF_CHUNK), BF16),
                        pltpu.VMEM((ROW_TILE, D_MODEL), F32)],
        compiler_params=_params("arbitrary", "arbitrary"),
        name="ffn_ln",
    )(x3, w_up, taps, w_dn, g, b)


def _pad_lanes(v):
    return jnp.pad(v.astype(F32), (0, LANES - v.shape[0])).reshape(1, LANES)


def kernel(x, mem, w_in, b_f, lambda_q1, lambda_k1, lambda_q2, lambda_k2, diff_norm_g, w_o,
           ln1_g, ln1_b, w_mq, w_mkv, w_mo, ln2_g, ln2_b, w_up, conv_w, conv_b, w_down,
           ln3_g, ln3_b):
    bsz, seq, _ = x.shape
    depth = w_in.shape[0]
    alpha = (2 * depth) ** 0.25
    diff_slopes, dil_slopes = _alibi_slopes()
    slope_pairs = jnp.asarray(
        np.repeat(dil_slopes.reshape(DIL_HEADS // 2, 2), HEAD_DIM, axis=1)
        .reshape(DIL_HEADS // 2, 1, LANES))
    ff_start = 3 * FOX_W
    row2 = lambda v: v.astype(F32).reshape(1, D_MODEL)

    x2 = x.reshape(bsz * seq, D_MODEL)
    for l in range(depth):
        lam_init = 0.8 - 0.6 * math.exp(-0.3 * l)
        w_main = jnp.concatenate(
            [w_in[l][:, :ff_start], w_in[l][:, ff_start + FOX_HEADS:]], axis=1).astype(BF16)
        w_ff = jnp.pad(w_in[l][:, ff_start:ff_start + FOX_HEADS],
                       ((0, 0), (0, LANES - FOX_HEADS))).astype(BF16)
        h2, f2 = _inproj(x2, w_main, w_ff)
        h3 = h2.reshape(bsz, seq, MAIN_COLS)
        fox_qa, fox_ka = _fox_prep(f2.reshape(bsz, seq, LANES), _pad_lanes(b_f[l]))
        fox = _fox_attention(h3, fox_qa, fox_ka)
        lamp = jnp.zeros((8, LANES), F32)
        for i, v in enumerate((lambda_q1, lambda_k1, lambda_q2, lambda_k2)):
            lamp = lamp.at[i, :DIFF_QK_DIM].set(v[l].astype(F32))
        gain_row = jnp.tile(diff_norm_g[l].astype(F32), 2).reshape(1, LANES)
        diff = _diff_attention(h3, lamp, gain_row, diff_slopes, lam_init)
        dil = _dil_attention(h3, slope_pairs)
        x2 = _outproj_ln(x2, fox.reshape(bsz * seq, FOX_W), diff.reshape(bsz * seq, DIFF_W),
                         dil.reshape(bsz * seq, DIL_W), w_o[l].astype(BF16),
                         row2(ln1_g[l]), row2(ln1_b[l]), alpha)
        kv = _memkv(mem, w_mkv[l].astype(BF16))
        x3 = _memattn_ln(x2.reshape(bsz, seq, D_MODEL), kv, w_mq[l].astype(BF16),
                         w_mo[l].astype(BF16), row2(ln2_g[l]), row2(ln2_b[l]), alpha)
        taps = jnp.concatenate([conv_w[l][:, 0, :], conv_b[l][None, :],
                                jnp.zeros((4, 2 * D_FF), F32)], axis=0).astype(F32)
        x3 = _ffn_ln(x3, w_up[l].astype(BF16), taps, w_down[l].astype(BF16),
                     row2(ln3_g[l]), row2(ln3_b[l]), alpha)
        x2 = x3.reshape(bsz * seq, D_MODEL)
    return x2.reshape(bsz, seq, D_MODEL)
```

```python
import functools
import math

import numpy as np
import jax
import jax.numpy as jnp
from jax import lax
from jax.experimental import pallas as pl
from jax.experimental.pallas import tpu as pltpu

F32 = jnp.float32
BF16 = jnp.bfloat16

D_MODEL = 1024
HEAD_DIM = 64
FOX_HEADS = 4
DIFF_HEADS = 4
DIL_HEADS = 8
FOX_W = FOX_HEADS * HEAD_DIM
DIFF_W = DIFF_HEADS * HEAD_DIM
DIL_W = DIL_HEADS * HEAD_DIM
MIX_W = FOX_W + DIFF_W + DIL_W
DIFF_QK_DIM = HEAD_DIM // 2
DIL_CONFIGS = ((128, 1), (512, 4), (2048, 16))
DIL_BLOCK = 128
MEM_HEADS = 4
MEM_HEAD_DIM = D_MODEL // MEM_HEADS
D_FF = 2816
LN_EPS = 1e-5
NEG_INF = -1e30

LANES = 128
VMEM_LIMIT = 56 * 1024 * 1024

ROW_TILE = 512
FF_ROW_TILE = 512
MIX_ROWS = 512
IN_CHUNK = 512
ATT_Q = 512
ATT_K = 256
FF_CHUNK = 256
N_FF_CHUNKS = D_FF // FF_CHUNK
FF_ROWS = 64
PREP_BLOCK = 256

COL_FQ, COL_FK, COL_FV, COL_DQ, COL_DK, COL_DV = 0, 1, 2, 3, 4, 5
COL_CQ, COL_CK, COL_CV = 12, 16, 20
MAIN_COLS = 3 * MIX_W


def _alibi_slopes():
    n = DIFF_HEADS + DIL_HEADS
    slopes = 2.0 ** (-8.0 * np.arange(1, n + 1) / n)
    stride = n // DIFF_HEADS
    diff_sel = np.arange(0, n, stride)[:DIFF_HEADS]
    dil_sel = np.setdiff1d(np.arange(n), diff_sel)
    return (np.asarray(slopes[diff_sel], np.float32), np.asarray(slopes[dil_sel], np.float32))


def _params(*sem):
    return pltpu.CompilerParams(dimension_semantics=sem, vmem_limit_bytes=VMEM_LIMIT)


def _layer_norm(y, g, b):
    mu = jnp.mean(y, axis=-1, keepdims=True)
    d = y - mu
    var = jnp.mean(d * d, axis=-1, keepdims=True)
    return d * lax.rsqrt(var + LN_EPS) * g + b


def _dot_nt(a, b):
    return lax.dot_general(a, b, (((1,), (1,)), ((), ())), preferred_element_type=F32)


def _inproj_weights_kernel(w_ref, main_ref, ff_ref):
    ff_lo, ff_hi = 3 * FOX_W, 3 * FOX_W + FOX_HEADS
    w = w_ref[0]
    main_ref[:, :ff_lo] = w[:, :ff_lo].astype(BF16)
    main_ref[:, ff_lo:] = w[:, ff_hi:].astype(BF16)
    ff = jnp.concatenate([w[:, ff_lo:ff_hi], jnp.zeros((w.shape[0], LANES - FOX_HEADS), F32)],
                         axis=1)
    ff_ref[...] = ff.astype(BF16)


def _inproj_weights(w_in, layer):
    rows = 256
    cols = w_in.shape[2]
    return pl.pallas_call(
        _inproj_weights_kernel,
        grid=(D_MODEL // rows,),
        in_specs=[pl.BlockSpec((1, rows, cols), lambda i: (layer, i, 0))],
        out_specs=[pl.BlockSpec((rows, MAIN_COLS), lambda i: (i, 0)),
                   pl.BlockSpec((rows, LANES), lambda i: (i, 0))],
        out_shape=[jax.ShapeDtypeStruct((D_MODEL, MAIN_COLS), BF16),
                   jax.ShapeDtypeStruct((D_MODEL, LANES), BF16)],
        compiler_params=_params("arbitrary"),
        name="inproj_weights",
    )(w_in)


def _inproj_kernel(x_ref, w_ref, wf_ref, h_ref, f_ref):
    xb = x_ref[...].astype(BF16)
    for n in range(0, MAIN_COLS, IN_CHUNK):
        h_ref[:, n:n + IN_CHUNK] = jnp.dot(
            xb, w_ref[:, n:n + IN_CHUNK], preferred_element_type=F32).astype(BF16)
    f_ref[...] = jnp.dot(xb, wf_ref[...], preferred_element_type=F32)


def _inproj(x2, w_main, w_ff):
    m = x2.shape[0]
    return pl.pallas_call(
        _inproj_kernel,
        grid=(m // ROW_TILE,),
        in_specs=[
            pl.BlockSpec((ROW_TILE, D_MODEL), lambda i: (i, 0)),
            pl.BlockSpec((D_MODEL, MAIN_COLS), lambda i: (0, 0)),
            pl.BlockSpec((D_MODEL, LANES), lambda i: (0, 0)),
        ],
        out_specs=[
            pl.BlockSpec((ROW_TILE, MAIN_COLS), lambda i: (i, 0)),
            pl.BlockSpec((ROW_TILE, LANES), lambda i: (i, 0)),
        ],
        out_shape=[
            jax.ShapeDtypeStruct((m, MAIN_COLS), BF16),
            jax.ShapeDtypeStruct((m, LANES), F32),
        ],
        compiler_params=_params("arbitrary"),
        name="inproj",
    )(x2, w_main, w_ff)


def _split3(x):
    hi = x.astype(BF16).astype(F32)
    r1 = x - hi
    mid = r1.astype(BF16).astype(F32)
    lo = (r1 - mid).astype(BF16).astype(F32)
    return hi, mid, lo


def _fox_prep_kernel(f_ref, bf_ref, qa_ref, ka_ref, *, seq):
    r = lax.broadcasted_iota(jnp.int32, (PREP_BLOCK, PREP_BLOCK), 0)
    c = lax.broadcasted_iota(jnp.int32, (PREP_BLOCK, PREP_BLOCK), 1)
    tri = jnp.where(c <= r, 1.0, 0.0).astype(BF16)
    lane = lax.broadcasted_iota(jnp.int32, (1, LANES), 1)

    def pick(pieces, base):
        return jnp.where(lane == base, pieces[0], jnp.where(lane == base + 1, pieces[1], pieces[2]))

    offset = jnp.zeros((1, LANES), F32)
    for blk in range(seq // PREP_BLOCK):
        rows = slice(blk * PREP_BLOCK, (blk + 1) * PREP_BLOCK)
        x = f_ref[0, rows, :] + bf_ref[...]
        logf = jnp.minimum(x, 0.0) - jnp.log1p(jnp.exp(-jnp.abs(x)))
        cs = sum(jnp.dot(tri, piece.astype(BF16), preferred_element_type=F32)
                 for piece in _split3(logf))
        cblk = cs + offset
        offset = cblk[PREP_BLOCK - 1:PREP_BLOCK, :]
        for pp in range(FOX_HEADS // 2):
            cols = slice(LANES * pp, LANES * (pp + 1))
            ce = _split3(jnp.broadcast_to(cblk[:, 2 * pp:2 * pp + 1], (PREP_BLOCK, LANES)))
            co = _split3(jnp.broadcast_to(cblk[:, 2 * pp + 1:2 * pp + 2], (PREP_BLOCK, LANES)))
            ka = jnp.where(lane < 3, 1.0,
                           jnp.where(lane < 6, -pick(ce, 3),
                                     jnp.where(lane < 9, 1.0,
                                               jnp.where(lane < 12, -pick(co, 9), 0.0))))
            qa_e = jnp.where(lane < 3, pick(ce, 0), jnp.where(lane < 6, 1.0, 0.0))
            qa_o = jnp.where(lane < 6, 0.0,
                             jnp.where(lane < 9, pick(co, 6), jnp.where(lane < 12, 1.0, 0.0)))
            ka_ref[0, rows, cols] = ka.astype(BF16)
            qa_ref[0, 0, rows, cols] = qa_e.astype(BF16)
            qa_ref[0, 1, rows, cols] = qa_o.astype(BF16)


def _fox_prep(f3, bf_row):
    b, seq, _ = f3.shape
    width = LANES * (FOX_HEADS // 2)
    return pl.pallas_call(
        functools.partial(_fox_prep_kernel, seq=seq),
        grid=(b,),
        in_specs=[
            pl.BlockSpec((1, seq, LANES), lambda i: (i, 0, 0)),
            pl.BlockSpec((1, LANES), lambda i: (0, 0)),
        ],
        out_specs=[
            pl.BlockSpec((1, 2, seq, width), lambda i: (i, 0, 0, 0)),
            pl.BlockSpec((1, seq, width), lambda i: (i, 0, 0)),
        ],
        out_shape=[
            jax.ShapeDtypeStruct((b, 2, seq, width), BF16),
            jax.ShapeDtypeStruct((b, seq, width), BF16),
        ],
        compiler_params=_params("arbitrary"),
        name="fox_prep",
    )(f3, bf_row)


def _causal_two_phase(q_aug, qi, key_tile, value_tile, log2_scale, s_sc, mp_sc, acc_sc):
    tq, tk = ATT_Q, ATT_K
    per_step = 2
    n_diag = tq // tk
    assert n_diag == per_step
    n_pairs = len(q_aug)
    rows = q_aug[0].shape[0]
    r = lax.broadcasted_iota(jnp.int32, (rows, tk), 0) & (tq - 1)
    c = lax.broadcasted_iota(jnp.int32, (rows, tk), 1)
    ones = jnp.ones((tk, LANES), BF16)
    mp_sc[...] = jnp.full(mp_sc.shape, NEG_INF, F32)

    def phase1(step, diagonal):
        for pp in range(n_pairs):
            part = mp_sc[pp]
            for i in range(per_step):
                k0 = pl.multiple_of((step * per_step + i) * tk, tk)
                s = _dot_nt(q_aug[pp], key_tile(pp, k0)) * log2_scale
                if diagonal:
                    s = jnp.where(c + i * tk <= r, s, NEG_INF)
                s_sc[pp, :, pl.ds(k0, tk)] = s
                part = jnp.maximum(part, jnp.maximum(s[:, :LANES], s[:, LANES:]))
            mp_sc[pp] = part

    def phase1_body(step, carry):
        phase1(step, False)
        return carry

    lax.fori_loop(0, qi, phase1_body, 0)
    phase1(qi, True)

    acc_sc[...] = jnp.zeros(acc_sc.shape, F32)
    row_max = [jnp.max(mp_sc[pp], axis=-1, keepdims=True) for pp in range(n_pairs)]

    def phase2_body(step, carry):
        for pp in range(n_pairs):
            pv = None
            for i in range(per_step):
                k0 = pl.multiple_of((step * per_step + i) * tk, tk)
                p = jnp.exp2(s_sc[pp, :, pl.ds(k0, tk)] - row_max[pp])
                d = jnp.dot(p.astype(BF16), jnp.concatenate([value_tile(pp, k0), ones], axis=1),
                            preferred_element_type=F32)
                pv = d if pv is None else pv + d
            acc_sc[pp] += pv
        return carry

    lax.fori_loop(0, qi + 1, phase2_body, 0)
    return [(acc_sc[pp, :, :LANES], acc_sc[pp, :, LANES:LANES + 1]) for pp in range(n_pairs)]


def _attn_scratch(n_pairs, rows, seq):
    return [pltpu.VMEM((n_pairs, rows, seq), F32), pltpu.VMEM((n_pairs, rows, LANES), F32),
            pltpu.VMEM((n_pairs, rows, 2 * LANES), F32)]


def _fox_kernel(q_ref, k_ref, v_ref, qa_ref, ka_ref, o_ref, s_sc, mp_sc, acc_sc):
    t = ATT_Q
    n_pairs = FOX_HEADS // 2
    qi = pl.program_id(1)
    lane = lax.broadcasted_iota(jnp.int32, (1, LANES), 1)
    even = lane < HEAD_DIM
    cols = [slice(LANES * pp, LANES * (pp + 1)) for pp in range(n_pairs)]
    q_aug = []
    for pp in range(n_pairs):
        qp = q_ref[0, :, cols[pp]].astype(F32) * (HEAD_DIM ** -0.5)
        heads = jnp.concatenate(
            [jnp.where(even, qp, 0.0), jnp.where(even, 0.0, qp)], axis=0).astype(BF16)
        bias = jnp.concatenate([qa_ref[0, 0, :, cols[pp]], qa_ref[0, 1, :, cols[pp]]], axis=0)
        q_aug.append(jnp.concatenate([heads, bias], axis=1))

    def key_tile(pp, k0):
        return jnp.concatenate([k_ref[0, pl.ds(k0, ATT_K), cols[pp]],
                                ka_ref[0, pl.ds(k0, ATT_K), cols[pp]]], axis=1)

    def value_tile(pp, k0):
        return v_ref[0, pl.ds(k0, ATT_K), cols[pp]]

    res = _causal_two_phase(q_aug, qi, key_tile, value_tile, math.log2(math.e),
                            s_sc, mp_sc, acc_sc)
    for pp in range(n_pairs):
        acc, l = res[pp]
        o = acc * (1.0 / l)
        o_ref[0, :, cols[pp]] = jnp.where(even, o[:t], o[t:]).astype(BF16)


def _fox_attention(h3, qa, ka):
    b, seq, _ = h3.shape
    t = ATT_Q
    return pl.pallas_call(
        _fox_kernel,
        grid=(b, seq // t),
        in_specs=[
            pl.BlockSpec((1, t, FOX_W), lambda i, j: (i, j, COL_FQ)),
            pl.BlockSpec((1, seq, FOX_W), lambda i, j: (i, 0, COL_FK)),
            pl.BlockSpec((1, seq, FOX_W), lambda i, j: (i, 0, COL_FV)),
            pl.BlockSpec((1, 2, t, FOX_W), lambda i, j: (i, 0, j, 0)),
            pl.BlockSpec((1, seq, FOX_W), lambda i, j: (i, 0, 0)),
        ],
        out_specs=pl.BlockSpec((1, t, FOX_W), lambda i, j: (i, j, 0)),
        out_shape=jax.ShapeDtypeStruct((b, seq, FOX_W), BF16),
        scratch_shapes=_attn_scratch(FOX_HEADS // 2, 2 * t, seq),
        compiler_params=_params("arbitrary", "arbitrary"),
        name="fox_attn",
    )(h3, h3, h3, qa, ka)


def _np_split3(x):
    hi = x.astype(BF16)
    r1 = x - hi.astype(np.float64)
    mid = r1.astype(BF16)
    lo = (r1 - mid.astype(np.float64)).astype(BF16)
    return hi, mid, lo


def _diff_alibi_columns(seq, slopes):
    scale = DIFF_QK_DIM ** -0.5
    pos = np.arange(seq)
    ka = np.zeros((seq, LANES), np.float32)
    ka[:, 0:3] = 1.0
    ka[:, 3:6] = (8 * (pos // 8))[:, None]
    ka[:, 6:9] = (pos % 8)[:, None]
    qa = np.zeros((DIFF_HEADS // 2, 4, seq, LANES), np.float32)
    for pp in range(DIFF_HEADS // 2):
        for mi in range(4):
            sl = float(slopes[2 * pp + mi // 2]) / scale
            a = _np_split3(-sl * pos.astype(np.float64))
            b = _np_split3(np.full(seq, sl, np.float64))
            for i in range(3):
                qa[pp, mi, :, i] = a[i].astype(np.float32)
                qa[pp, mi, :, 3 + i] = b[i].astype(np.float32)
                qa[pp, mi, :, 6 + i] = b[i].astype(np.float32)
    return jnp.asarray(qa, BF16), jnp.asarray(ka, BF16)


def _diff_kernel(q_ref, k_ref, v_ref, qa_ref, ka_ref, lamp_ref, g_ref, o_ref,
                 s_sc, mp_sc, acc_sc, *, lam_init):
    t = ATT_Q
    qi = pl.program_id(2)
    lane = lax.broadcasted_iota(jnp.int32, (1, LANES), 1)
    lamp = lamp_ref[...]
    a1 = jnp.sum(lamp[0:1] * lamp[1:2], axis=-1, keepdims=True)
    a2 = jnp.sum(lamp[2:3] * lamp[3:4], axis=-1, keepdims=True)
    lam = jnp.exp(a1) - jnp.exp(a2) + lam_init
    gain = g_ref[...]
    qp = q_ref[0].astype(F32)
    groups = []
    for mi in range(4):
        sel = (lane >= DIFF_QK_DIM * mi) & (lane < DIFF_QK_DIM * (mi + 1))
        groups.append(jnp.where(sel, qp, 0.0))
    maps = jnp.concatenate(groups, axis=0).astype(BF16)
    q_aug = [jnp.concatenate([maps, qa_ref[0].reshape(4 * t, LANES)], axis=1)]

    def key_tile(pp, k0):
        return jnp.concatenate([k_ref[0, pl.ds(k0, ATT_K), :], ka_ref[pl.ds(k0, ATT_K), :]],
                               axis=1)

    def value_tile(pp, k0):
        return v_ref[0, pl.ds(k0, ATT_K), :]

    (acc, l), = _causal_two_phase(q_aug, qi, key_tile, value_tile,
                                  DIFF_QK_DIM ** -0.5 * math.log2(math.e), s_sc, mp_sc, acc_sc)
    o = acc * (1.0 / l)
    outs = []
    for hh in range(2):
        oh = o[(2 * hh) * t:(2 * hh + 1) * t] - lam * o[(2 * hh + 1) * t:(2 * hh + 2) * t]
        half = (lane >= HEAD_DIM * hh) & (lane < HEAD_DIM * (hh + 1))
        ms = jnp.sum(jnp.where(half, oh * oh, 0.0), axis=-1, keepdims=True) * (1.0 / HEAD_DIM)
        outs.append(oh * lax.rsqrt(ms + LN_EPS) * gain * (1.0 - lam_init))
    o_ref[0] = jnp.where(lane < HEAD_DIM, outs[0], outs[1]).astype(BF16)


def _diff_attention(h3, lamp, gain_row, slopes, lam_init):
    b, seq, _ = h3.shape
    t = ATT_Q
    n_pairs = DIFF_HEADS // 2
    qa, ka = _diff_alibi_columns(seq, slopes)
    q0, k0, v0 = (c * (DIFF_W // LANES) for c in (COL_DQ, COL_DK, COL_DV))
    return pl.pallas_call(
        functools.partial(_diff_kernel, lam_init=lam_init),
        grid=(b, n_pairs, seq // t),
        in_specs=[
            pl.BlockSpec((1, t, LANES), lambda i, p, j: (i, j, q0 + p)),
            pl.BlockSpec((1, seq, LANES), lambda i, p, j: (i, 0, k0 + p)),
            pl.BlockSpec((1, seq, LANES), lambda i, p, j: (i, 0, v0 + p)),
            pl.BlockSpec((1, 4, t, LANES), lambda i, p, j: (p, 0, j, 0)),
            pl.BlockSpec((seq, LANES), lambda i, p, j: (0, 0)),
            pl.BlockSpec((8, LANES), lambda i, p, j: (0, 0)),
            pl.BlockSpec((1, LANES), lambda i, p, j: (0, 0)),
        ],
        out_specs=pl.BlockSpec((1, t, LANES), lambda i, p, j: (i, j, p)),
        out_shape=jax.ShapeDtypeStruct((b, seq, DIFF_W), BF16),
        scratch_shapes=_attn_scratch(1, 4 * t, seq),
        compiler_params=_params("arbitrary", "arbitrary", "arbitrary"),
        name="diff_attn",
    )(h3, h3, h3, qa, ka, lamp, gain_row)


def _dil_kernel(q_ref, k_ref, v_ref, slope_ref, o_ref,
                qf, kf, vf, kp, vp, m_st, l_st, a_st, *, seq):
    blk = DIL_BLOCK
    nblk = seq // blk
    lane = lax.broadcasted_iota(jnp.int32, (1, LANES), 1)
    even = lane < HEAD_DIM
    slope = slope_ref[0]
    nslope_e = -slope[:, 0:1]
    nslope_o = -slope[:, HEAD_DIM:HEAD_DIM + 1]
    qi = lax.broadcasted_iota(jnp.int32, (blk, 2 * blk), 0)
    ki = lax.broadcasted_iota(jnp.int32, (blk, 2 * blk), 1) - blk
    delta = qi - ki
    valid = (delta >= 0) & (delta <= blk)
    deltaf = delta.astype(F32)

    qf[...] = q_ref[0].astype(F32) * (HEAD_DIM ** -0.5)
    kf[...] = k_ref[0].astype(F32)
    vf[...] = v_ref[0].astype(F32)

    def block(rows, kw, vw, bias, merge):
        qb = qf[rows, :]
        q_stack = jnp.concatenate(
            [jnp.where(even, qb, 0.0), jnp.where(even, 0.0, qb)], axis=0).astype(BF16)
        s = _dot_nt(q_stack, kw) + bias
        m = jnp.max(s, axis=-1, keepdims=True)
        p = jnp.exp(s - m)
        l = jnp.sum(p, axis=-1, keepdims=True)
        acc = jnp.dot(p.astype(BF16), vw, preferred_element_type=F32)
        m_b = jnp.where(even, m[:blk], m[blk:])
        l_b = jnp.where(even, l[:blk], l[blk:])
        a_b = jnp.where(even, acc[:blk], acc[blk:])
        if merge:
            m_old = m_st[rows, :]
            m_new = jnp.maximum(m_old, m_b)
            w_old = jnp.exp(m_old - m_new)
            w_blk = jnp.exp(m_b - m_new)
            m_st[rows, :] = m_new
            l_st[rows, :] = w_old * l_st[rows, :] + w_blk * l_b
            a_st[rows, :] = w_old * a_st[rows, :] + w_blk * a_b
        else:
            m_st[rows, :] = m_b
            l_st[rows, :] = l_b
            a_st[rows, :] = a_b

    for _, dil in DIL_CONFIGS:
        per_res = nblk // dil
        length = seq // dil
        dist = deltaf * float(dil)
        bias = jnp.concatenate([jnp.where(valid, nslope_e * dist, NEG_INF),
                                jnp.where(valid, nslope_o * dist, NEG_INF)], axis=0)
        bias_first = bias[:, blk:]

        for b in range(nblk):
            res, n = divmod(b, per_res)
            win = slice((b - 1 if n else b) * blk, (b + 1) * blk)
            if dil == 1:
                block(pl.ds(b * blk, blk), k_ref[0, win, :], v_ref[0, win, :],
                      bias if n else bias_first, False)
                continue
            if n == 0:
                dst = slice(res * length, (res + 1) * length)
                kp[dst, :] = kf[pl.ds(res, length, stride=dil), :].astype(BF16)
                vp[dst, :] = vf[pl.ds(res, length, stride=dil), :].astype(BF16)
            block(pl.ds(dil * blk * n + res, blk, stride=dil), kp[win, :], vp[win, :],
                  bias if n else bias_first, True)

    o_ref[0] = (a_st[...] * (1.0 / l_st[...])).astype(BF16)


def _dil_attention(h3, slope_pairs):
    b, seq, _ = h3.shape
    npairs = DIL_HEADS // 2
    f32_buf = pltpu.VMEM((seq, LANES), F32)
    bf16_buf = pltpu.VMEM((seq, LANES), BF16)
    return pl.pallas_call(
        functools.partial(_dil_kernel, seq=seq),
        grid=(b, npairs),
        in_specs=[
            pl.BlockSpec((1, seq, LANES), lambda i, p: (i, 0, COL_CQ + p)),
            pl.BlockSpec((1, seq, LANES), lambda i, p: (i, 0, COL_CK + p)),
            pl.BlockSpec((1, seq, LANES), lambda i, p: (i, 0, COL_CV + p)),
            pl.BlockSpec((1, 1, LANES), lambda i, p: (p, 0, 0)),
        ],
        out_specs=pl.BlockSpec((1, seq, LANES), lambda i, p: (i, 0, p)),
        out_shape=jax.ShapeDtypeStruct((b, seq, DIL_W), BF16),
        scratch_shapes=[f32_buf, f32_buf, f32_buf, bf16_buf, bf16_buf,
                        f32_buf, f32_buf, f32_buf],
        compiler_params=_params("arbitrary", "arbitrary"),
        name="dil_attn",
    )(h3, h3, h3, slope_pairs)


def _memkv_kernel(m_ref, w_ref, o_ref):
    o_ref[0] = jnp.dot(m_ref[0].astype(BF16), w_ref[...],
                       preferred_element_type=F32).astype(BF16)


def _memkv(mem, w_kv):
    b, mt, _ = mem.shape
    return pl.pallas_call(
        _memkv_kernel,
        grid=(b,),
        in_specs=[pl.BlockSpec((1, mt, D_MODEL), lambda i: (i, 0, 0)),
                  pl.BlockSpec((D_MODEL, 2 * D_MODEL), lambda i: (0, 0))],
        out_specs=pl.BlockSpec((1, mt, 2 * D_MODEL), lambda i: (i, 0, 0)),
        out_shape=jax.ShapeDtypeStruct((b, mt, 2 * D_MODEL), BF16),
        compiler_params=_params("arbitrary"),
        name="memkv",
    )(mem, w_kv)


def _mix_mem_kernel(x_ref, fox_ref, diff_ref, dil_ref, wmix_ref, g1_ref, b1_ref,
                    kv_ref, wq_ref, wo_ref, g2_ref, b2_ref, o_ref, *, alpha):
    for r in range(0, ROW_TILE, MIX_ROWS):
        rows = slice(r, r + MIX_ROWS)
        mix = (jnp.dot(fox_ref[0, rows, :], wmix_ref[0:FOX_W, :], preferred_element_type=F32)
               + jnp.dot(diff_ref[0, rows, :], wmix_ref[FOX_W:FOX_W + DIFF_W, :],
                         preferred_element_type=F32)
               + jnp.dot(dil_ref[0, rows, :], wmix_ref[FOX_W + DIFF_W:, :],
                         preferred_element_type=F32))
        x1 = _layer_norm(alpha * x_ref[0, rows, :] + mix, g1_ref[...], b1_ref[...])
        q = jnp.dot(x1.astype(BF16), wq_ref[...], preferred_element_type=F32)
        q = (q * (MEM_HEAD_DIM ** -0.5)).astype(BF16)
        y = alpha * x1
        for h in range(MEM_HEADS):
            cols = slice(MEM_HEAD_DIM * h, MEM_HEAD_DIM * (h + 1))
            vcols = slice(D_MODEL + MEM_HEAD_DIM * h, D_MODEL + MEM_HEAD_DIM * (h + 1))
            s = _dot_nt(q[:, cols], kv_ref[0, :, cols])
            m = jnp.max(s, axis=-1, keepdims=True)
            p = jnp.exp(s - m)
            l = jnp.sum(p, axis=-1, keepdims=True)
            o = jnp.dot(p.astype(BF16), kv_ref[0, :, vcols],
                        preferred_element_type=F32) * (1.0 / l)
            y = y + jnp.dot(o.astype(BF16), wo_ref[cols, :], preferred_element_type=F32)
        o_ref[0, rows, :] = _layer_norm(y, g2_ref[...], b2_ref[...])


def _mix_mem_ln(x3, fox, diff, dil, w_mix, g1, b1, kv, w_q, w_o, g2, b2, alpha):
    bsz, seq, _ = x3.shape
    mt = kv.shape[1]
    row = lambda w: pl.BlockSpec((1, ROW_TILE, w), lambda i, j: (i, j, 0))
    const = lambda r, w: pl.BlockSpec((r, w), lambda i, j: (0, 0))
    return pl.pallas_call(
        functools.partial(_mix_mem_kernel, alpha=alpha),
        grid=(bsz, seq // ROW_TILE),
        in_specs=[
            row(D_MODEL), row(FOX_W), row(DIFF_W), row(DIL_W),
            const(MIX_W, D_MODEL), const(1, D_MODEL), const(1, D_MODEL),
            pl.BlockSpec((1, mt, 2 * D_MODEL), lambda i, j: (i, 0, 0)),
            const(D_MODEL, D_MODEL), const(D_MODEL, D_MODEL),
            const(1, D_MODEL), const(1, D_MODEL),
        ],
        out_specs=row(D_MODEL),
        out_shape=jax.ShapeDtypeStruct((bsz, seq, D_MODEL), F32),
        compiler_params=_params("arbitrary", "arbitrary"),
        name="mix_mem_ln",
    )(x3, fox, diff, dil, w_mix, g1, b1, kv, w_q, w_o, g2, b2)


def _ffn_kernel(x_ref, wup_ref, taps_ref, wdn_ref, g_ref, b_ref, o_ref,
                carry_ref, u_ref, h_ref, acc_ref, *, alpha):
    tm = FF_ROW_TILE
    n_slabs = 2 * FF_CHUNK // LANES

    @pl.when(pl.program_id(1) == 0)
    def _():
        carry_ref[...] = jnp.zeros_like(carry_ref)

    x = x_ref[0]
    xb = x.astype(BF16)
    acc_ref[...] = alpha * x

    def slab_cols(c, slab):
        part, half = divmod(slab, n_slabs // 2)
        lo = part * D_FF + c * FF_CHUNK + half * LANES
        return slice(lo, lo + LANES)

    def up_project(c):
        slot = c % 2
        for part in range(2):
            lo = part * D_FF + c * FF_CHUNK
            u = jnp.dot(xb, wup_ref[:, lo:lo + FF_CHUNK], preferred_element_type=F32)
            for half in range(n_slabs // 2):
                slab = part * (n_slabs // 2) + half
                piece = u[:, half * LANES:(half + 1) * LANES]
                u_ref[slot, slab, 0:8, :] = carry_ref[n_slabs * c + slab]
                u_ref[slot, slab, 8:8 + tm, :] = piece
                carry_ref[n_slabs * c + slab] = piece[tm - 8:tm, :]

    def conv_glu(c):
        slot = c % 2
        taps = [taps_ref[:, slab_cols(c, slab)] for slab in range(n_slabs)]
        for r in range(0, tm, FF_ROWS):
            ys = []
            for slab in range(n_slabs):
                t = taps[slab]
                ys.append(t[0:1] * u_ref[slot, slab, 6 + r:6 + r + FF_ROWS, :]
                          + t[1:2] * u_ref[slot, slab, 7 + r:7 + r + FF_ROWS, :]
                          + t[2:3] * u_ref[slot, slab, 8 + r:8 + r + FF_ROWS, :] + t[3:4])
            for half in range(n_slabs // 2):
                h = jax.nn.gelu(ys[half]) * ys[n_slabs // 2 + half]
                h_ref[slot, r:r + FF_ROWS, half * LANES:(half + 1) * LANES] = h.astype(BF16)

    def down_project(c):
        acc_ref[...] += jnp.dot(h_ref[c % 2], wdn_ref[c * FF_CHUNK:(c + 1) * FF_CHUNK, :],
                                preferred_element_type=F32)

    up_project(0)
    for c in range(N_FF_CHUNKS):
        if c + 1 < N_FF_CHUNKS:
            up_project(c + 1)
        conv_glu(c)
        down_project(c)
    o_ref[0] = _layer_norm(acc_ref[...], g_ref[...], b_ref[...])


def _ffn_ln(x3, w_up, taps, w_dn, g, b, alpha):
    bsz, seq, _ = x3.shape
    const = lambda r, w: pl.BlockSpec((r, w), lambda i, j: (0, 0))
    return pl.pallas_call(
        functools.partial(_ffn_kernel, alpha=alpha),
        grid=(bsz, seq // FF_ROW_TILE),
        in_specs=[
            pl.BlockSpec((1, FF_ROW_TILE, D_MODEL), lambda i, j: (i, j, 0)),
            const(D_MODEL, 2 * D_FF), const(8, 2 * D_FF), const(D_FF, D_MODEL),
            const(1, D_MODEL), const(1, D_MODEL),
        ],
        out_specs=pl.BlockSpec((1, FF_ROW_TILE, D_MODEL), lambda i, j: (i, j, 0)),
        out_shape=jax.ShapeDtypeStruct((bsz, seq, D_MODEL), F32),
        scratch_shapes=[pltpu.VMEM((2 * D_FF // LANES, 8, LANES), F32),
                        pltpu.VMEM((2, 2 * FF_CHUNK // LANES, 8 + FF_ROW_TILE, LANES), F32),
                        pltpu.VMEM((2, FF_ROW_TILE, FF_CHUNK), BF16),
                        pltpu.VMEM((FF_ROW_TILE, D_MODEL), F32)],
        compiler_params=_params("arbitrary", "arbitrary"),
        name="ffn_ln",
    )(x3, w_up, taps, w_dn, g, b)


def _pad_lanes(v):
    return jnp.pad(v.astype(F32), (0, LANES - v.shape[0])).reshape(1, LANES)


def kernel(x, mem, w_in, b_f, lambda_q1, lambda_k1, lambda_q2, lambda_k2, diff_norm_g, w_o,
           ln1_g, ln1_b, w_mq, w_mkv, w_mo, ln2_g, ln2_b, w_up, conv_w, conv_b, w_down,
           ln3_g, ln3_b):
    bsz, seq, _ = x.shape
    depth = w_in.shape[0]
    alpha = (2 * depth) ** 0.25
    diff_slopes, dil_slopes = _alibi_slopes()
    slope_pairs = jnp.asarray(
        np.repeat(dil_slopes.reshape(DIL_HEADS // 2, 2), HEAD_DIM, axis=1)
        .reshape(DIL_HEADS // 2, 1, LANES))
    row2 = lambda v: v.astype(F32).reshape(1, D_MODEL)

    x2 = x.reshape(bsz * seq, D_MODEL)
    for l in range(depth):
        lam_init = 0.8 - 0.6 * math.exp(-0.3 * l)
        w_main, w_ff = _inproj_weights(w_in, l)
        h2, f2 = _inproj(x2, w_main, w_ff)
        h3 = h2.reshape(bsz, seq, MAIN_COLS)
        fox_qa, fox_ka = _fox_prep(f2.reshape(bsz, seq, LANES), _pad_lanes(b_f[l]))
        fox = _fox_attention(h3, fox_qa, fox_ka)
        lamp = jnp.zeros((8, LANES), F32)
        for i, v in enumerate((lambda_q1, lambda_k1, lambda_q2, lambda_k2)):
            lamp = lamp.at[i, :DIFF_QK_DIM].set(v[l].astype(F32))
        gain_row = jnp.tile(diff_norm_g[l].astype(F32), 2).reshape(1, LANES)
        diff = _diff_attention(h3, lamp, gain_row, diff_slopes, lam_init)
        dil = _dil_attention(h3, slope_pairs)
        kv = _memkv(mem, w_mkv[l].astype(BF16))
        x3 = _mix_mem_ln(x2.reshape(bsz, seq, D_MODEL), fox, diff, dil, w_o[l].astype(BF16),
                         row2(ln1_g[l]), row2(ln1_b[l]), kv, w_mq[l].astype(BF16),
                         w_mo[l].astype(BF16), row2(ln2_g[l]), row2(ln2_b[l]), alpha)
        taps = jnp.concatenate([conv_w[l][:, 0, :], conv_b[l][None, :],
                                jnp.zeros((4, 2 * D_FF), F32)], axis=0).astype(F32)
        x3 = _ffn_ln(x3, w_up[l].astype(BF16), taps, w_down[l].astype(BF16),
                     row2(ln3_g[l]), row2(ln3_b[l]), alpha)
        x2 = x3.reshape(bsz * seq, D_MODEL)
    return x2.reshape(bsz, seq, D_MODEL)
```

```python
import functools
import math

import numpy as np
import jax
import jax.numpy as jnp
from jax import lax
from jax.experimental import pallas as pl
from jax.experimental.pallas import tpu as pltpu

F32 = jnp.float32
BF16 = jnp.bfloat16

D_MODEL = 1024
HEAD_DIM = 64
FOX_HEADS = 4
DIFF_HEADS = 4
DIL_HEADS = 8
FOX_W = FOX_HEADS * HEAD_DIM
DIFF_W = DIFF_HEADS * HEAD_DIM
DIL_W = DIL_HEADS * HEAD_DIM
MIX_W = FOX_W + DIFF_W + DIL_W
DIFF_QK_DIM = HEAD_DIM // 2
DIL_CONFIGS = ((128, 1), (512, 4), (2048, 16))
DIL_BLOCK = 128
MEM_HEADS = 4
MEM_HEAD_DIM = D_MODEL // MEM_HEADS
D_FF = 2816
LN_EPS = 1e-5
NEG_INF = -1e30

LANES = 128
VMEM_LIMIT = 56 * 1024 * 1024

ROW_TILE = 512
FF_ROW_TILE = 512
MIX_ROW_TILE = 1024
MIX_ROWS = 1024
IN_CHUNK = 512
ATT_Q = 512
ATT_K = 256
ATT_GROUP = 4
FF_CHUNK = 256
N_FF_CHUNKS = D_FF // FF_CHUNK
FF_ROWS = 64
PREP_BLOCK = 256

COL_FQ, COL_FK, COL_FV, COL_DQ, COL_DK, COL_DV = 0, 1, 2, 3, 4, 5
COL_CQ, COL_CK, COL_CV = 12, 16, 20
MAIN_COLS = 3 * MIX_W


def _alibi_slopes():
    n = DIFF_HEADS + DIL_HEADS
    slopes = 2.0 ** (-8.0 * np.arange(1, n + 1) / n)
    stride = n // DIFF_HEADS
    diff_sel = np.arange(0, n, stride)[:DIFF_HEADS]
    dil_sel = np.setdiff1d(np.arange(n), diff_sel)
    return (np.asarray(slopes[diff_sel], np.float32), np.asarray(slopes[dil_sel], np.float32))


def _params(*sem):
    return pltpu.CompilerParams(dimension_semantics=sem, vmem_limit_bytes=VMEM_LIMIT)


def _layer_norm(y, g, b):
    mu = jnp.mean(y, axis=-1, keepdims=True)
    d = y - mu
    var = jnp.mean(d * d, axis=-1, keepdims=True)
    return d * lax.rsqrt(var + LN_EPS) * g + b


def _dot_nt(a, b):
    return lax.dot_general(a, b, (((1,), (1,)), ((), ())), preferred_element_type=F32)


def _inproj_weights_kernel(w_ref, main_ref, ff_ref):
    ff_lo, ff_hi = 3 * FOX_W, 3 * FOX_W + FOX_HEADS
    w = w_ref[0]
    main_ref[:, :ff_lo] = w[:, :ff_lo].astype(BF16)
    main_ref[:, ff_lo:] = w[:, ff_hi:].astype(BF16)
    ff = jnp.concatenate([w[:, ff_lo:ff_hi], jnp.zeros((w.shape[0], LANES - FOX_HEADS), F32)],
                         axis=1)
    ff_ref[...] = ff.astype(BF16)


def _inproj_weights(w_in, layer):
    rows = 256
    cols = w_in.shape[2]
    return pl.pallas_call(
        _inproj_weights_kernel,
        grid=(D_MODEL // rows,),
        in_specs=[pl.BlockSpec((1, rows, cols), lambda i: (layer, i, 0))],
        out_specs=[pl.BlockSpec((rows, MAIN_COLS), lambda i: (i, 0)),
                   pl.BlockSpec((rows, LANES), lambda i: (i, 0))],
        out_shape=[jax.ShapeDtypeStruct((D_MODEL, MAIN_COLS), BF16),
                   jax.ShapeDtypeStruct((D_MODEL, LANES), BF16)],
        compiler_params=_params("arbitrary"),
        name="inproj_weights",
    )(w_in)


def _inproj_kernel(x_ref, w_ref, wf_ref, h_ref, f_ref):
    xb = x_ref[...].astype(BF16)
    for n in range(0, MAIN_COLS, IN_CHUNK):
        h_ref[:, n:n + IN_CHUNK] = jnp.dot(
            xb, w_ref[:, n:n + IN_CHUNK], preferred_element_type=F32).astype(BF16)
    f_ref[...] = jnp.dot(xb, wf_ref[...], preferred_element_type=F32)


def _inproj(x2, w_main, w_ff):
    m = x2.shape[0]
    return pl.pallas_call(
        _inproj_kernel,
        grid=(m // ROW_TILE,),
        in_specs=[
            pl.BlockSpec((ROW_TILE, D_MODEL), lambda i: (i, 0)),
            pl.BlockSpec((D_MODEL, MAIN_COLS), lambda i: (0, 0)),
            pl.BlockSpec((D_MODEL, LANES), lambda i: (0, 0)),
        ],
        out_specs=[
            pl.BlockSpec((ROW_TILE, MAIN_COLS), lambda i: (i, 0)),
            pl.BlockSpec((ROW_TILE, LANES), lambda i: (i, 0)),
        ],
        out_shape=[
            jax.ShapeDtypeStruct((m, MAIN_COLS), BF16),
            jax.ShapeDtypeStruct((m, LANES), F32),
        ],
        compiler_params=_params("arbitrary"),
        name="inproj",
    )(x2, w_main, w_ff)


def _split3(x):
    hi = x.astype(BF16).astype(F32)
    r1 = x - hi
    mid = r1.astype(BF16).astype(F32)
    lo = (r1 - mid).astype(BF16).astype(F32)
    return hi, mid, lo


def _fox_prep_kernel(f_ref, bf_ref, qa_ref, ka_ref, *, seq):
    r = lax.broadcasted_iota(jnp.int32, (PREP_BLOCK, PREP_BLOCK), 0)
    c = lax.broadcasted_iota(jnp.int32, (PREP_BLOCK, PREP_BLOCK), 1)
    tri = jnp.where(c <= r, 1.0, 0.0).astype(BF16)
    lane = lax.broadcasted_iota(jnp.int32, (1, LANES), 1)

    def pick(pieces, base):
        return jnp.where(lane == base, pieces[0], jnp.where(lane == base + 1, pieces[1], pieces[2]))

    offset = jnp.zeros((1, LANES), F32)
    for blk in range(seq // PREP_BLOCK):
        rows = slice(blk * PREP_BLOCK, (blk + 1) * PREP_BLOCK)
        x = f_ref[0, rows, :] + bf_ref[...]
        logf = jnp.minimum(x, 0.0) - jnp.log1p(jnp.exp(-jnp.abs(x)))
        cs = sum(jnp.dot(tri, piece.astype(BF16), preferred_element_type=F32)
                 for piece in _split3(logf))
        cblk = cs + offset
        offset = cblk[PREP_BLOCK - 1:PREP_BLOCK, :]
        for pp in range(FOX_HEADS // 2):
            cols = slice(LANES * pp, LANES * (pp + 1))
            ce = _split3(jnp.broadcast_to(cblk[:, 2 * pp:2 * pp + 1], (PREP_BLOCK, LANES)))
            co = _split3(jnp.broadcast_to(cblk[:, 2 * pp + 1:2 * pp + 2], (PREP_BLOCK, LANES)))
            ka = jnp.where(lane < 3, 1.0,
                           jnp.where(lane < 6, -pick(ce, 3),
                                     jnp.where(lane < 9, 1.0,
                                               jnp.where(lane < 12, -pick(co, 9), 0.0))))
            qa_e = jnp.where(lane < 3, pick(ce, 0), jnp.where(lane < 6, 1.0, 0.0))
            qa_o = jnp.where(lane < 6, 0.0,
                             jnp.where(lane < 9, pick(co, 6), jnp.where(lane < 12, 1.0, 0.0)))
            ka_ref[0, rows, cols] = ka.astype(BF16)
            qa_ref[0, 0, rows, cols] = qa_e.astype(BF16)
            qa_ref[0, 1, rows, cols] = qa_o.astype(BF16)


def _fox_prep(f3, bf_row):
    b, seq, _ = f3.shape
    width = LANES * (FOX_HEADS // 2)
    return pl.pallas_call(
        functools.partial(_fox_prep_kernel, seq=seq),
        grid=(b,),
        in_specs=[
            pl.BlockSpec((1, seq, LANES), lambda i: (i, 0, 0)),
            pl.BlockSpec((1, LANES), lambda i: (0, 0)),
        ],
        out_specs=[
            pl.BlockSpec((1, 2, seq, width), lambda i: (i, 0, 0, 0)),
            pl.BlockSpec((1, seq, width), lambda i: (i, 0, 0)),
        ],
        out_shape=[
            jax.ShapeDtypeStruct((b, 2, seq, width), BF16),
            jax.ShapeDtypeStruct((b, seq, width), BF16),
        ],
        compiler_params=_params("arbitrary"),
        name="fox_prep",
    )(f3, bf_row)


def _causal_two_phase(q_aug, qi, key_tile, value_tile, log2_scale, s_sc, mp_sc, acc_sc):
    tq, tk = ATT_Q, ATT_K
    per_step = 2
    n_diag = tq // tk
    assert n_diag == per_step
    n_pairs = len(q_aug)
    rows = q_aug[0].shape[0]
    r = lax.broadcasted_iota(jnp.int32, (rows, tk), 0) & (tq - 1)
    c = lax.broadcasted_iota(jnp.int32, (rows, tk), 1)
    ones = jnp.ones((tk, LANES), BF16)
    mp_sc[...] = jnp.full(mp_sc.shape, NEG_INF, F32)

    def phase1(first_tile, count, diagonal):
        for pp in range(n_pairs):
            part = mp_sc[pp]
            for i in range(count):
                k0 = pl.multiple_of((first_tile + i) * tk, tk)
                s = _dot_nt(q_aug[pp], key_tile(pp, k0)) * log2_scale
                if diagonal:
                    s = jnp.where(c + i * tk <= r, s, NEG_INF)
                s_sc[pp, :, pl.ds(k0, tk)] = s
                part = jnp.maximum(part, jnp.maximum(s[:, :LANES], s[:, LANES:]))
            mp_sc[pp] = part

    def phase2(first_tile, count):
        for pp in range(n_pairs):
            pv = None
            for i in range(count):
                k0 = pl.multiple_of((first_tile + i) * tk, tk)
                p = jnp.exp2(s_sc[pp, :, pl.ds(k0, tk)] - row_max[pp])
                d = jnp.dot(p.astype(BF16), jnp.concatenate([value_tile(pp, k0), ones], axis=1),
                            preferred_element_type=F32)
                pv = d if pv is None else pv + d
            acc_sc[pp] += pv

    def run(phase, n_tiles):
        long_steps = n_tiles // ATT_GROUP

        def long_body(g, carry):
            phase(g * ATT_GROUP, ATT_GROUP)
            return carry

        def short_body(g, carry):
            phase(long_steps * ATT_GROUP + g * per_step, per_step)
            return carry

        lax.fori_loop(0, long_steps, long_body, 0)
        lax.fori_loop(0, (n_tiles - long_steps * ATT_GROUP) // per_step, short_body, 0)

    n_full = qi * n_diag
    run(functools.partial(phase1, diagonal=False), n_full)
    phase1(n_full, n_diag, True)

    acc_sc[...] = jnp.zeros(acc_sc.shape, F32)
    row_max = [jnp.max(mp_sc[pp], axis=-1, keepdims=True) for pp in range(n_pairs)]
    run(phase2, n_full + n_diag)
    return [(acc_sc[pp, :, :LANES], acc_sc[pp, :, LANES:LANES + 1]) for pp in range(n_pairs)]


def _attn_scratch(n_pairs, rows, seq):
    return [pltpu.VMEM((n_pairs, rows, seq), F32), pltpu.VMEM((n_pairs, rows, LANES), F32),
            pltpu.VMEM((n_pairs, rows, 2 * LANES), F32)]


def _fox_kernel(q_ref, k_ref, v_ref, qa_ref, ka_ref, o_ref, s_sc, mp_sc, acc_sc):
    t = ATT_Q
    n_pairs = FOX_HEADS // 2
    qi = pl.program_id(1)
    lane = lax.broadcasted_iota(jnp.int32, (1, LANES), 1)
    even = lane < HEAD_DIM
    cols = [slice(LANES * pp, LANES * (pp + 1)) for pp in range(n_pairs)]
    q_aug = []
    for pp in range(n_pairs):
        qp = q_ref[0, :, cols[pp]].astype(F32) * (HEAD_DIM ** -0.5)
        heads = jnp.concatenate(
            [jnp.where(even, qp, 0.0), jnp.where(even, 0.0, qp)], axis=0).astype(BF16)
        bias = jnp.concatenate([qa_ref[0, 0, :, cols[pp]], qa_ref[0, 1, :, cols[pp]]], axis=0)
        q_aug.append(jnp.concatenate([heads, bias], axis=1))

    def key_tile(pp, k0):
        return jnp.concatenate([k_ref[0, pl.ds(k0, ATT_K), cols[pp]],
                                ka_ref[0, pl.ds(k0, ATT_K), cols[pp]]], axis=1)

    def value_tile(pp, k0):
        return v_ref[0, pl.ds(k0, ATT_K), cols[pp]]

    res = _causal_two_phase(q_aug, qi, key_tile, value_tile, math.log2(math.e),
                            s_sc, mp_sc, acc_sc)
    for pp in range(n_pairs):
        acc, l = res[pp]
        o = acc * (1.0 / l)
        o_ref[0, :, cols[pp]] = jnp.where(even, o[:t], o[t:]).astype(BF16)


def _fox_attention(h3, qa, ka):
    b, seq, _ = h3.shape
    t = ATT_Q
    return pl.pallas_call(
        _fox_kernel,
        grid=(b, seq // t),
        in_specs=[
            pl.BlockSpec((1, t, FOX_W), lambda i, j: (i, j, COL_FQ)),
            pl.BlockSpec((1, seq, FOX_W), lambda i, j: (i, 0, COL_FK)),
            pl.BlockSpec((1, seq, FOX_W), lambda i, j: (i, 0, COL_FV)),
            pl.BlockSpec((1, 2, t, FOX_W), lambda i, j: (i, 0, j, 0)),
            pl.BlockSpec((1, seq, FOX_W), lambda i, j: (i, 0, 0)),
        ],
        out_specs=pl.BlockSpec((1, t, FOX_W), lambda i, j: (i, j, 0)),
        out_shape=jax.ShapeDtypeStruct((b, seq, FOX_W), BF16),
        scratch_shapes=_attn_scratch(FOX_HEADS // 2, 2 * t, seq),
        compiler_params=_params("arbitrary", "arbitrary"),
        name="fox_attn",
    )(h3, h3, h3, qa, ka)


def _np_split3(x):
    hi = x.astype(BF16)
    r1 = x - hi.astype(np.float64)
    mid = r1.astype(BF16)
    lo = (r1 - mid.astype(np.float64)).astype(BF16)
    return hi, mid, lo


def _diff_alibi_columns(seq, slopes):
    scale = DIFF_QK_DIM ** -0.5
    pos = np.arange(seq)
    ka = np.zeros((seq, LANES), np.float32)
    ka[:, 0:3] = 1.0
    ka[:, 3:6] = (8 * (pos // 8))[:, None]
    ka[:, 6:9] = (pos % 8)[:, None]
    qa = np.zeros((DIFF_HEADS // 2, 4, seq, LANES), np.float32)
    for pp in range(DIFF_HEADS // 2):
        for mi in range(4):
            sl = float(slopes[2 * pp + mi // 2]) / scale
            a = _np_split3(-sl * pos.astype(np.float64))
            b = _np_split3(np.full(seq, sl, np.float64))
            for i in range(3):
                qa[pp, mi, :, i] = a[i].astype(np.float32)
                qa[pp, mi, :, 3 + i] = b[i].astype(np.float32)
                qa[pp, mi, :, 6 + i] = b[i].astype(np.float32)
    return jnp.asarray(qa, BF16), jnp.asarray(ka, BF16)


def _diff_kernel(q_ref, k_ref, v_ref, qa_ref, ka_ref, lamp_ref, g_ref, o_ref,
                 s_sc, mp_sc, acc_sc, *, lam_init):
    t = ATT_Q
    qi = pl.program_id(2)
    lane = lax.broadcasted_iota(jnp.int32, (1, LANES), 1)
    lamp = lamp_ref[...]
    a1 = jnp.sum(lamp[0:1] * lamp[1:2], axis=-1, keepdims=True)
    a2 = jnp.sum(lamp[2:3] * lamp[3:4], axis=-1, keepdims=True)
    lam = jnp.exp(a1) - jnp.exp(a2) + lam_init
    gain = g_ref[...]
    qp = q_ref[0].astype(F32)
    groups = []
    for mi in range(4):
        sel = (lane >= DIFF_QK_DIM * mi) & (lane < DIFF_QK_DIM * (mi + 1))
        groups.append(jnp.where(sel, qp, 0.0))
    maps = jnp.concatenate(groups, axis=0).astype(BF16)
    q_aug = [jnp.concatenate([maps, qa_ref[0].reshape(4 * t, LANES)], axis=1)]

    def key_tile(pp, k0):
        return jnp.concatenate([k_ref[0, pl.ds(k0, ATT_K), :], ka_ref[pl.ds(k0, ATT_K), :]],
                               axis=1)

    def value_tile(pp, k0):
        return v_ref[0, pl.ds(k0, ATT_K), :]

    (acc, l), = _causal_two_phase(q_aug, qi, key_tile, value_tile,
                                  DIFF_QK_DIM ** -0.5 * math.log2(math.e), s_sc, mp_sc, acc_sc)
    o = acc * (1.0 / l)
    outs = []
    for hh in range(2):
        oh = o[(2 * hh) * t:(2 * hh + 1) * t] - lam * o[(2 * hh + 1) * t:(2 * hh + 2) * t]
        half = (lane >= HEAD_DIM * hh) & (lane < HEAD_DIM * (hh + 1))
        ms = jnp.sum(jnp.where(half, oh * oh, 0.0), axis=-1, keepdims=True) * (1.0 / HEAD_DIM)
        outs.append(oh * lax.rsqrt(ms + LN_EPS) * gain * (1.0 - lam_init))
    o_ref[0] = jnp.where(lane < HEAD_DIM, outs[0], outs[1]).astype(BF16)


def _diff_attention(h3, lamp, gain_row, slopes, lam_init):
    b, seq, _ = h3.shape
    t = ATT_Q
    n_pairs = DIFF_HEADS // 2
    qa, ka = _diff_alibi_columns(seq, slopes)
    q0, k0, v0 = (c * (DIFF_W // LANES) for c in (COL_DQ, COL_DK, COL_DV))
    return pl.pallas_call(
        functools.partial(_diff_kernel, lam_init=lam_init),
        grid=(b, n_pairs, seq // t),
        in_specs=[
            pl.BlockSpec((1, t, LANES), lambda i, p, j: (i, j, q0 + p)),
            pl.BlockSpec((1, seq, LANES), lambda i, p, j: (i, 0, k0 + p)),
            pl.BlockSpec((1, seq, LANES), lambda i, p, j: (i, 0, v0 + p)),
            pl.BlockSpec((1, 4, t, LANES), lambda i, p, j: (p, 0, j, 0)),
            pl.BlockSpec((seq, LANES), lambda i, p, j: (0, 0)),
            pl.BlockSpec((8, LANES), lambda i, p, j: (0, 0)),
            pl.BlockSpec((1, LANES), lambda i, p, j: (0, 0)),
        ],
        out_specs=pl.BlockSpec((1, t, LANES), lambda i, p, j: (i, j, p)),
        out_shape=jax.ShapeDtypeStruct((b, seq, DIFF_W), BF16),
        scratch_shapes=_attn_scratch(1, 4 * t, seq),
        compiler_params=_params("arbitrary", "arbitrary", "arbitrary"),
        name="diff_attn",
    )(h3, h3, h3, qa, ka, lamp, gain_row)


def _dil_kernel(q_ref, k_ref, v_ref, slope_ref, o_ref,
                qf, kf, vf, kp, vp, m_st, l_st, a_st, *, seq):
    blk = DIL_BLOCK
    nblk = seq // blk
    lane = lax.broadcasted_iota(jnp.int32, (1, LANES), 1)
    even = lane < HEAD_DIM
    slope = slope_ref[0]
    nslope_e = -slope[:, 0:1]
    nslope_o = -slope[:, HEAD_DIM:HEAD_DIM + 1]
    qi = lax.broadcasted_iota(jnp.int32, (blk, 2 * blk), 0)
    ki = lax.broadcasted_iota(jnp.int32, (blk, 2 * blk), 1) - blk
    delta = qi - ki
    valid = (delta >= 0) & (delta <= blk)
    deltaf = delta.astype(F32)

    qf[...] = q_ref[0].astype(F32) * (HEAD_DIM ** -0.5)
    kf[...] = k_ref[0].astype(F32)
    vf[...] = v_ref[0].astype(F32)

    def block(rows, kw, vw, bias, merge):
        qb = qf[rows, :]
        q_stack = jnp.concatenate(
            [jnp.where(even, qb, 0.0), jnp.where(even, 0.0, qb)], axis=0).astype(BF16)
        s = _dot_nt(q_stack, kw) + bias
        m = jnp.max(s, axis=-1, keepdims=True)
        p = jnp.exp(s - m)
        l = jnp.sum(p, axis=-1, keepdims=True)
        acc = jnp.dot(p.astype(BF16), vw, preferred_element_type=F32)
        m_b = jnp.where(even, m[:blk], m[blk:])
        l_b = jnp.where(even, l[:blk], l[blk:])
        a_b = jnp.where(even, acc[:blk], acc[blk:])
        if merge:
            m_old = m_st[rows, :]
            m_new = jnp.maximum(m_old, m_b)
            w_old = jnp.exp(m_old - m_new)
            w_blk = jnp.exp(m_b - m_new)
            m_st[rows, :] = m_new
            l_st[rows, :] = w_old * l_st[rows, :] + w_blk * l_b
            a_st[rows, :] = w_old * a_st[rows, :] + w_blk * a_b
        else:
            m_st[rows, :] = m_b
            l_st[rows, :] = l_b
            a_st[rows, :] = a_b

    for _, dil in DIL_CONFIGS:
        per_res = nblk // dil
        length = seq // dil
        dist = deltaf * float(dil)
        bias = jnp.concatenate([jnp.where(valid, nslope_e * dist, NEG_INF),
                                jnp.where(valid, nslope_o * dist, NEG_INF)], axis=0)
        bias_first = bias[:, blk:]

        for b in range(nblk):
            res, n = divmod(b, per_res)
            win = slice((b - 1 if n else b) * blk, (b + 1) * blk)
            if dil == 1:
                block(pl.ds(b * blk, blk), k_ref[0, win, :], v_ref[0, win, :],
                      bias if n else bias_first, False)
                continue
            if n == 0:
                dst = slice(res * length, (res + 1) * length)
                kp[dst, :] = kf[pl.ds(res, length, stride=dil), :].astype(BF16)
                vp[dst, :] = vf[pl.ds(res, length, stride=dil), :].astype(BF16)
            block(pl.ds(dil * blk * n + res, blk, stride=dil), kp[win, :], vp[win, :],
                  bias if n else bias_first, True)

    o_ref[0] = (a_st[...] * (1.0 / l_st[...])).astype(BF16)


def _dil_attention(h3, slope_pairs):
    b, seq, _ = h3.shape
    npairs = DIL_HEADS // 2
    f32_buf = pltpu.VMEM((seq, LANES), F32)
    bf16_buf = pltpu.VMEM((seq, LANES), BF16)
    return pl.pallas_call(
        functools.partial(_dil_kernel, seq=seq),
        grid=(b, npairs),
        in_specs=[
            pl.BlockSpec((1, seq, LANES), lambda i, p: (i, 0, COL_CQ + p)),
            pl.BlockSpec((1, seq, LANES), lambda i, p: (i, 0, COL_CK + p)),
            pl.BlockSpec((1, seq, LANES), lambda i, p: (i, 0, COL_CV + p)),
            pl.BlockSpec((1, 1, LANES), lambda i, p: (p, 0, 0)),
        ],
        out_specs=pl.BlockSpec((1, seq, LANES), lambda i, p: (i, 0, p)),
        out_shape=jax.ShapeDtypeStruct((b, seq, DIL_W), BF16),
        scratch_shapes=[f32_buf, f32_buf, f32_buf, bf16_buf, bf16_buf,
                        f32_buf, f32_buf, f32_buf],
        compiler_params=_params("arbitrary", "arbitrary"),
        name="dil_attn",
    )(h3, h3, h3, slope_pairs)


def _memkv_kernel(m_ref, w_ref, o_ref):
    o_ref[0] = jnp.dot(m_ref[0].astype(BF16), w_ref[...],
                       preferred_element_type=F32).astype(BF16)


def _memkv(mem, w_kv):
    b, mt, _ = mem.shape
    return pl.pallas_call(
        _memkv_kernel,
        grid=(b,),
        in_specs=[pl.BlockSpec((1, mt, D_MODEL), lambda i: (i, 0, 0)),
                  pl.BlockSpec((D_MODEL, 2 * D_MODEL), lambda i: (0, 0))],
        out_specs=pl.BlockSpec((1, mt, 2 * D_MODEL), lambda i: (i, 0, 0)),
        out_shape=jax.ShapeDtypeStruct((b, mt, 2 * D_MODEL), BF16),
        compiler_params=_params("arbitrary"),
        name="memkv",
    )(mem, w_kv)


def _mix_mem_kernel(x_ref, fox_ref, diff_ref, dil_ref, wmix_ref, g1_ref, b1_ref,
                    kv_ref, wq_ref, wo_ref, g2_ref, b2_ref, o_ref, *, alpha):
    for r in range(0, MIX_ROW_TILE, MIX_ROWS):
        rows = slice(r, r + MIX_ROWS)
        mix = (jnp.dot(fox_ref[0, rows, :], wmix_ref[0:FOX_W, :], preferred_element_type=F32)
               + jnp.dot(diff_ref[0, rows, :], wmix_ref[FOX_W:FOX_W + DIFF_W, :],
                         preferred_element_type=F32)
               + jnp.dot(dil_ref[0, rows, :], wmix_ref[FOX_W + DIFF_W:, :],
                         preferred_element_type=F32))
        x1 = _layer_norm(alpha * x_ref[0, rows, :] + mix, g1_ref[...], b1_ref[...])
        q = jnp.dot(x1.astype(BF16), wq_ref[...], preferred_element_type=F32)
        q = (q * (MEM_HEAD_DIM ** -0.5)).astype(BF16)
        y = alpha * x1
        for h in range(MEM_HEADS):
            cols = slice(MEM_HEAD_DIM * h, MEM_HEAD_DIM * (h + 1))
            vcols = slice(D_MODEL + MEM_HEAD_DIM * h, D_MODEL + MEM_HEAD_DIM * (h + 1))
            s = _dot_nt(q[:, cols], kv_ref[0, :, cols])
            m = jnp.max(s, axis=-1, keepdims=True)
            p = jnp.exp(s - m)
            l = jnp.sum(p, axis=-1, keepdims=True)
            o = jnp.dot(p.astype(BF16), kv_ref[0, :, vcols],
                        preferred_element_type=F32) * (1.0 / l)
            y = y + jnp.dot(o.astype(BF16), wo_ref[cols, :], preferred_element_type=F32)
        o_ref[0, rows, :] = _layer_norm(y, g2_ref[...], b2_ref[...])


def _mix_mem_ln(x3, fox, diff, dil, w_mix, g1, b1, kv, w_q, w_o, g2, b2, alpha):
    bsz, seq, _ = x3.shape
    mt = kv.shape[1]
    row = lambda w: pl.BlockSpec((1, MIX_ROW_TILE, w), lambda i, j: (i, j, 0))
    const = lambda r, w: pl.BlockSpec((r, w), lambda i, j: (0, 0))
    return pl.pallas_call(
        functools.partial(_mix_mem_kernel, alpha=alpha),
        grid=(bsz, seq // MIX_ROW_TILE),
        in_specs=[
            row(D_MODEL), row(FOX_W), row(DIFF_W), row(DIL_W),
            const(MIX_W, D_MODEL), const(1, D_MODEL), const(1, D_MODEL),
            pl.BlockSpec((1, mt, 2 * D_MODEL), lambda i, j: (i, 0, 0)),
            const(D_MODEL, D_MODEL), const(D_MODEL, D_MODEL),
            const(1, D_MODEL), const(1, D_MODEL),
        ],
        out_specs=row(D_MODEL),
        out_shape=jax.ShapeDtypeStruct((bsz, seq, D_MODEL), F32),
        compiler_params=_params("arbitrary", "arbitrary"),
        name="mix_mem_ln",
    )(x3, fox, diff, dil, w_mix, g1, b1, kv, w_q, w_o, g2, b2)


def _ffn_kernel(x_ref, wup_ref, taps_ref, wdn_ref, g_ref, b_ref, o_ref,
                carry_ref, u_ref, h_ref, acc_ref, *, alpha):
    tm = FF_ROW_TILE
    n_slabs = 2 * FF_CHUNK // LANES

    @pl.when(pl.program_id(1) == 0)
    def _():
        carry_ref[...] = jnp.zeros_like(carry_ref)

    x = x_ref[0]
    xb = x.astype(BF16)
    acc_ref[...] = alpha * x

    def slab_cols(c, slab):
        part, half = divmod(slab, n_slabs // 2)
        lo = part * D_FF + c * FF_CHUNK + half * LANES
        return slice(lo, lo + LANES)

    def up_project(c):
        slot = c % 2
        for part in range(2):
            lo = part * D_FF + c * FF_CHUNK
            u = jnp.dot(xb, wup_ref[:, lo:lo + FF_CHUNK], preferred_element_type=F32)
            for half in range(n_slabs // 2):
                slab = part * (n_slabs // 2) + half
                piece = u[:, half * LANES:(half + 1) * LANES]
                u_ref[slot, slab, 0:8, :] = carry_ref[n_slabs * c + slab]
                u_ref[slot, slab, 8:8 + tm, :] = piece
                carry_ref[n_slabs * c + slab] = piece[tm - 8:tm, :]

    def conv_glu(c):
        slot = c % 2
        taps = [taps_ref[:, slab_cols(c, slab)] for slab in range(n_slabs)]
        for r in range(0, tm, FF_ROWS):
            ys = []
            for slab in range(n_slabs):
                t = taps[slab]
                ys.append(t[0:1] * u_ref[slot, slab, 6 + r:6 + r + FF_ROWS, :]
                          + t[1:2] * u_ref[slot, slab, 7 + r:7 + r + FF_ROWS, :]
                          + t[2:3] * u_ref[slot, slab, 8 + r:8 + r + FF_ROWS, :] + t[3:4])
            for half in range(n_slabs // 2):
                h = jax.nn.gelu(ys[half]) * ys[n_slabs // 2 + half]
                h_ref[slot, r:r + FF_ROWS, half * LANES:(half + 1) * LANES] = h.astype(BF16)

    def down_project(c):
        acc_ref[...] += jnp.dot(h_ref[c % 2], wdn_ref[c * FF_CHUNK:(c + 1) * FF_CHUNK, :],
                                preferred_element_type=F32)

    up_project(0)
    for c in range(N_FF_CHUNKS):
        if c + 1 < N_FF_CHUNKS:
            up_project(c + 1)
        conv_glu(c)
        down_project(c)
    o_ref[0] = _layer_norm(acc_ref[...], g_ref[...], b_ref[...])


def _ffn_ln(x3, w_up, taps, w_dn, g, b, alpha):
    bsz, seq, _ = x3.shape
    const = lambda r, w: pl.BlockSpec((r, w), lambda i, j: (0, 0))
    return pl.pallas_call(
        functools.partial(_ffn_kernel, alpha=alpha),
        grid=(bsz, seq // FF_ROW_TILE),
        in_specs=[
            pl.BlockSpec((1, FF_ROW_TILE, D_MODEL), lambda i, j: (i, j, 0)),
            const(D_MODEL, 2 * D_FF), const(8, 2 * D_FF), const(D_FF, D_MODEL),
            const(1, D_MODEL), const(1, D_MODEL),
        ],
        out_specs=pl.BlockSpec((1, FF_ROW_TILE, D_MODEL), lambda i, j: (i, j, 0)),
        out_shape=jax.ShapeDtypeStruct((bsz, seq, D_MODEL), F32),
        scratch_shapes=[pltpu.VMEM((2 * D_FF // LANES, 8, LANES), F32),
                        pltpu.VMEM((2, 2 * FF_CHUNK // LANES, 8 + FF_ROW_TILE, LANES), F32),
                        pltpu.VMEM((2, FF_ROW_TILE, FF_CHUNK), BF16),
                        pltpu.VMEM((FF_ROW_TILE, D_MODEL), F32)],
        compiler_params=_params("arbitrary", "arbitrary"),
        name="ffn_ln",
    )(x3, w_up, taps, w_dn, g, b)


def _pad_lanes(v):
    return jnp.pad(v.astype(F32), (0, LANES - v.shape[0])).reshape(1, LANES)


def kernel(x, mem, w_in, b_f, lambda_q1, lambda_k1, lambda_q2, lambda_k2, diff_norm_g, w_o,
           ln1_g, ln1_b, w_mq, w_mkv, w_mo, ln2_g, ln2_b, w_up, conv_w, conv_b, w_down,
           ln3_g, ln3_b):
    bsz, seq, _ = x.shape
    depth = w_in.shape[0]
    alpha = (2 * depth) ** 0.25
    diff_slopes, dil_slopes = _alibi_slopes()
    slope_pairs = jnp.asarray(
        np.repeat(dil_slopes.reshape(DIL_HEADS // 2, 2), HEAD_DIM, axis=1)
        .reshape(DIL_HEADS // 2, 1, LANES))
    row2 = lambda v: v.astype(F32).reshape(1, D_MODEL)

    x2 = x.reshape(bsz * seq, D_MODEL)
    for l in range(depth):
        lam_init = 0.8 - 0.6 * math.exp(-0.3 * l)
        w_main, w_ff = _inproj_weights(w_in, l)
        h2, f2 = _inproj(x2, w_main, w_ff)
        h3 = h2.reshape(bsz, seq, MAIN_COLS)
        fox_qa, fox_ka = _fox_prep(f2.reshape(bsz, seq, LANES), _pad_lanes(b_f[l]))
        fox = _fox_attention(h3, fox_qa, fox_ka)
        lamp = jnp.zeros((8, LANES), F32)
        for i, v in enumerate((lambda_q1, lambda_k1, lambda_q2, lambda_k2)):
            lamp = lamp.at[i, :DIFF_QK_DIM].set(v[l].astype(F32))
        gain_row = jnp.tile(diff_norm_g[l].astype(F32), 2).reshape(1, LANES)
        diff = _diff_attention(h3, lamp, gain_row, diff_slopes, lam_init)
        dil = _dil_attention(h3, slope_pairs)
        kv = _memkv(mem, w_mkv[l].astype(BF16))
        x3 = _mix_mem_ln(x2.reshape(bsz, seq, D_MODEL), fox, diff, dil, w_o[l].astype(BF16),
                         row2(ln1_g[l]), row2(ln1_b[l]), kv, w_mq[l].astype(BF16),
                         w_mo[l].astype(BF16), row2(ln2_g[l]), row2(ln2_b[l]), alpha)
        taps = jnp.concatenate([conv_w[l][:, 0, :], conv_b[l][None, :],
                                jnp.zeros((4, 2 * D_FF), F32)], axis=0).astype(F32)
        x3 = _ffn_ln(x3, w_up[l].astype(BF16), taps, w_down[l].astype(BF16),
                     row2(ln3_g[l]), row2(ln3_b[l]), alpha)
        x2 = x3.reshape(bsz * seq, D_MODEL)
    return x2.reshape(bsz, seq, D_MODEL)
```

```python
import functools
import math

import numpy as np
import jax
import jax.numpy as jnp
from jax import lax
from jax.experimental import pallas as pl
from jax.experimental.pallas import tpu as pltpu

F32 = jnp.float32
BF16 = jnp.bfloat16

D_MODEL = 1024
HEAD_DIM = 64
FOX_HEADS = 4
DIFF_HEADS = 4
DIL_HEADS = 8
FOX_W = FOX_HEADS * HEAD_DIM
DIFF_W = DIFF_HEADS * HEAD_DIM
DIL_W = DIL_HEADS * HEAD_DIM
MIX_W = FOX_W + DIFF_W + DIL_W
DIFF_QK_DIM = HEAD_DIM // 2
DIL_CONFIGS = ((128, 1), (512, 4), (2048, 16))
DIL_BLOCK = 128
MEM_HEADS = 4
MEM_HEAD_DIM = D_MODEL // MEM_HEADS
D_FF = 2816
LN_EPS = 1e-5
NEG_INF = -1e30

LANES = 128
VMEM_LIMIT = 56 * 1024 * 1024

ROW_TILE = 512
FF_ROW_TILE = 512
MIX_ROW_TILE = 1024
MIX_ROWS = 1024
IN_CHUNK = 512
ATT_Q = 512
ATT_K = 256
ATT_GROUP = 4
FF_CHUNK = 256
N_FF_CHUNKS = D_FF // FF_CHUNK
FF_ROWS = 64
PREP_BLOCK = 256

COL_FQ, COL_FK, COL_FV, COL_DQ, COL_DK, COL_DV = 0, 1, 2, 3, 4, 5
COL_CQ, COL_CK, COL_CV = 12, 16, 20
MAIN_COLS = 3 * MIX_W


def _alibi_slopes():
    n = DIFF_HEADS + DIL_HEADS
    slopes = 2.0 ** (-8.0 * np.arange(1, n + 1) / n)
    stride = n // DIFF_HEADS
    diff_sel = np.arange(0, n, stride)[:DIFF_HEADS]
    dil_sel = np.setdiff1d(np.arange(n), diff_sel)
    return (np.asarray(slopes[diff_sel], np.float32), np.asarray(slopes[dil_sel], np.float32))


def _params(*sem):
    return pltpu.CompilerParams(dimension_semantics=sem, vmem_limit_bytes=VMEM_LIMIT)


def _layer_norm(y, g, b):
    mu = jnp.mean(y, axis=-1, keepdims=True)
    d = y - mu
    var = jnp.mean(d * d, axis=-1, keepdims=True)
    return d * lax.rsqrt(var + LN_EPS) * g + b


def _dot_nt(a, b):
    return lax.dot_general(a, b, (((1,), (1,)), ((), ())), preferred_element_type=F32)


def _inproj_weights_kernel(w_ref, main_ref, ff_ref):
    ff_lo, ff_hi = 3 * FOX_W, 3 * FOX_W + FOX_HEADS
    w = w_ref[0]
    main_ref[:, :ff_lo] = w[:, :ff_lo].astype(BF16)
    main_ref[:, ff_lo:] = w[:, ff_hi:].astype(BF16)
    ff = jnp.concatenate([w[:, ff_lo:ff_hi], jnp.zeros((w.shape[0], LANES - FOX_HEADS), F32)],
                         axis=1)
    ff_ref[...] = ff.astype(BF16)


def _inproj_weights(w_in, layer):
    rows = 256
    cols = w_in.shape[2]
    return pl.pallas_call(
        _inproj_weights_kernel,
        grid=(D_MODEL // rows,),
        in_specs=[pl.BlockSpec((1, rows, cols), lambda i: (layer, i, 0))],
        out_specs=[pl.BlockSpec((rows, MAIN_COLS), lambda i: (i, 0)),
                   pl.BlockSpec((rows, LANES), lambda i: (i, 0))],
        out_shape=[jax.ShapeDtypeStruct((D_MODEL, MAIN_COLS), BF16),
                   jax.ShapeDtypeStruct((D_MODEL, LANES), BF16)],
        compiler_params=_params("arbitrary"),
        name="inproj_weights",
    )(w_in)


def _inproj_kernel(x_ref, w_ref, wf_ref, h_ref, f_ref):
    xb = x_ref[...].astype(BF16)
    for n in range(0, MAIN_COLS, IN_CHUNK):
        h_ref[:, n:n + IN_CHUNK] = jnp.dot(
            xb, w_ref[:, n:n + IN_CHUNK], preferred_element_type=F32).astype(BF16)
    f_ref[...] = jnp.dot(xb, wf_ref[...], preferred_element_type=F32)


def _inproj(x2, w_main, w_ff):
    m = x2.shape[0]
    return pl.pallas_call(
        _inproj_kernel,
        grid=(m // ROW_TILE,),
        in_specs=[
            pl.BlockSpec((ROW_TILE, D_MODEL), lambda i: (i, 0)),
            pl.BlockSpec((D_MODEL, MAIN_COLS), lambda i: (0, 0)),
            pl.BlockSpec((D_MODEL, LANES), lambda i: (0, 0)),
        ],
        out_specs=[
            pl.BlockSpec((ROW_TILE, MAIN_COLS), lambda i: (i, 0)),
            pl.BlockSpec((ROW_TILE, LANES), lambda i: (i, 0)),
        ],
        out_shape=[
            jax.ShapeDtypeStruct((m, MAIN_COLS), BF16),
            jax.ShapeDtypeStruct((m, LANES), F32),
        ],
        compiler_params=_params("arbitrary"),
        name="inproj",
    )(x2, w_main, w_ff)


def _split3(x):
    hi = x.astype(BF16).astype(F32)
    r1 = x - hi
    mid = r1.astype(BF16).astype(F32)
    lo = (r1 - mid).astype(BF16).astype(F32)
    return hi, mid, lo


def _fox_bias_placement():
    width = LANES * (FOX_HEADS // 2)
    place = np.zeros((3, 3 * LANES, width), np.float32)
    ones = np.zeros((8, width), np.float32)
    for pp in range(FOX_HEADS // 2):
        base = LANES * pp
        for i in range(3):
            place[0, LANES * i + 2 * pp, base + i] = 1.0
            place[1, LANES * i + 2 * pp + 1, base + 6 + i] = 1.0
            place[2, LANES * i + 2 * pp, base + 3 + i] = -1.0
            place[2, LANES * i + 2 * pp + 1, base + 9 + i] = -1.0
        ones[0, base + 3:base + 6] = 1.0
        ones[1, base + 9:base + 12] = 1.0
        ones[2, base:base + 3] = 1.0
        ones[2, base + 6:base + 9] = 1.0
    return jnp.asarray(place, BF16), jnp.asarray(ones, F32)


def _fox_prep_kernel(f_ref, bf_ref, place_ref, ones_ref, qa_ref, ka_ref, *, seq):
    r = lax.broadcasted_iota(jnp.int32, (PREP_BLOCK, PREP_BLOCK), 0)
    c = lax.broadcasted_iota(jnp.int32, (PREP_BLOCK, PREP_BLOCK), 1)
    tri = jnp.where(c <= r, 1.0, 0.0).astype(BF16)
    offset = jnp.zeros((1, LANES), F32)
    for blk in range(seq // PREP_BLOCK):
        rows = slice(blk * PREP_BLOCK, (blk + 1) * PREP_BLOCK)
        x = f_ref[0, rows, :] + bf_ref[...]
        logf = jnp.minimum(x, 0.0) - jnp.log1p(jnp.exp(-jnp.abs(x)))
        cs = sum(jnp.dot(tri, piece.astype(BF16), preferred_element_type=F32)
                 for piece in _split3(logf))
        cblk = cs + offset
        offset = cblk[PREP_BLOCK - 1:PREP_BLOCK, :]
        pieces = jnp.concatenate(_split3(cblk), axis=1).astype(BF16)
        placed = [jnp.dot(pieces, place_ref[i], preferred_element_type=F32) + ones_ref[i:i + 1, :]
                  for i in range(3)]
        qa_ref[0, 0, rows, :] = placed[0].astype(BF16)
        qa_ref[0, 1, rows, :] = placed[1].astype(BF16)
        ka_ref[0, rows, :] = placed[2].astype(BF16)


def _fox_prep(f3, bf_row):
    b, seq, _ = f3.shape
    width = LANES * (FOX_HEADS // 2)
    place, ones = _fox_bias_placement()
    return pl.pallas_call(
        functools.partial(_fox_prep_kernel, seq=seq),
        grid=(b,),
        in_specs=[
            pl.BlockSpec((1, seq, LANES), lambda i: (i, 0, 0)),
            pl.BlockSpec((1, LANES), lambda i: (0, 0)),
            pl.BlockSpec((3, 3 * LANES, width), lambda i: (0, 0, 0)),
            pl.BlockSpec((8, width), lambda i: (0, 0)),
        ],
        out_specs=[
            pl.BlockSpec((1, 2, seq, width), lambda i: (i, 0, 0, 0)),
            pl.BlockSpec((1, seq, width), lambda i: (i, 0, 0)),
        ],
        out_shape=[
            jax.ShapeDtypeStruct((b, 2, seq, width), BF16),
            jax.ShapeDtypeStruct((b, seq, width), BF16),
        ],
        compiler_params=_params("arbitrary"),
        name="fox_prep",
    )(f3, bf_row, place, ones)


def _causal_two_phase(q_aug, qi, key_tile, value_tile, log2_scale, s_sc, mp_sc, acc_sc):
    tq, tk = ATT_Q, ATT_K
    per_step = 2
    n_diag = tq // tk
    assert n_diag == per_step
    n_pairs = len(q_aug)
    rows = q_aug[0].shape[0]
    r = lax.broadcasted_iota(jnp.int32, (rows, tk), 0) & (tq - 1)
    c = lax.broadcasted_iota(jnp.int32, (rows, tk), 1)
    ones = jnp.ones((tk, LANES), BF16)
    mp_sc[...] = jnp.full(mp_sc.shape, NEG_INF, F32)

    def phase1(first_tile, count, diagonal):
        for pp in range(n_pairs):
            part = mp_sc[pp]
            for i in range(count):
                k0 = pl.multiple_of((first_tile + i) * tk, tk)
                s = _dot_nt(q_aug[pp], key_tile(pp, k0)) * log2_scale
                if diagonal:
                    s = jnp.where(c + i * tk <= r, s, NEG_INF)
                s_sc[pp, :, pl.ds(k0, tk)] = s
                part = jnp.maximum(part, jnp.maximum(s[:, :LANES], s[:, LANES:]))
            mp_sc[pp] = part

    def phase2(first_tile, count):
        for pp in range(n_pairs):
            pv = None
            for i in range(count):
                k0 = pl.multiple_of((first_tile + i) * tk, tk)
                p = jnp.exp2(s_sc[pp, :, pl.ds(k0, tk)] - row_max[pp])
                d = jnp.dot(p.astype(BF16), jnp.concatenate([value_tile(pp, k0), ones], axis=1),
                            preferred_element_type=F32)
                pv = d if pv is None else pv + d
            acc_sc[pp] += pv

    def run(phase, n_tiles):
        long_steps = n_tiles // ATT_GROUP

        def long_body(g, carry):
            phase(g * ATT_GROUP, ATT_GROUP)
            return carry

        def short_body(g, carry):
            phase(long_steps * ATT_GROUP + g * per_step, per_step)
            return carry

        lax.fori_loop(0, long_steps, long_body, 0)
        lax.fori_loop(0, (n_tiles - long_steps * ATT_GROUP) // per_step, short_body, 0)

    n_full = qi * n_diag
    run(functools.partial(phase1, diagonal=False), n_full)
    phase1(n_full, n_diag, True)

    acc_sc[...] = jnp.zeros(acc_sc.shape, F32)
    row_max = [jnp.max(mp_sc[pp], axis=-1, keepdims=True) for pp in range(n_pairs)]
    run(phase2, n_full + n_diag)
    return [(acc_sc[pp, :, :LANES], acc_sc[pp, :, LANES:LANES + 1]) for pp in range(n_pairs)]


def _attn_scratch(n_pairs, rows, seq):
    return [pltpu.VMEM((n_pairs, rows, seq), F32), pltpu.VMEM((n_pairs, rows, LANES), F32),
            pltpu.VMEM((n_pairs, rows, 2 * LANES), F32)]


def _fox_kernel(q_ref, k_ref, v_ref, qa_ref, ka_ref, o_ref, s_sc, mp_sc, acc_sc):
    t = ATT_Q
    n_pairs = FOX_HEADS // 2
    qi = pl.program_id(1)
    lane = lax.broadcasted_iota(jnp.int32, (1, LANES), 1)
    even = lane < HEAD_DIM
    cols = [slice(LANES * pp, LANES * (pp + 1)) for pp in range(n_pairs)]
    q_aug = []
    for pp in range(n_pairs):
        qp = q_ref[0, :, cols[pp]].astype(F32) * (HEAD_DIM ** -0.5)
        heads = jnp.concatenate(
            [jnp.where(even, qp, 0.0), jnp.where(even, 0.0, qp)], axis=0).astype(BF16)
        bias = jnp.concatenate([qa_ref[0, 0, :, cols[pp]], qa_ref[0, 1, :, cols[pp]]], axis=0)
        q_aug.append(jnp.concatenate([heads, bias], axis=1))

    def key_tile(pp, k0):
        return jnp.concatenate([k_ref[0, pl.ds(k0, ATT_K), cols[pp]],
                                ka_ref[0, pl.ds(k0, ATT_K), cols[pp]]], axis=1)

    def value_tile(pp, k0):
        return v_ref[0, pl.ds(k0, ATT_K), cols[pp]]

    res = _causal_two_phase(q_aug, qi, key_tile, value_tile, math.log2(math.e),
                            s_sc, mp_sc, acc_sc)
    for pp in range(n_pairs):
        acc, l = res[pp]
        o = acc * (1.0 / l)
        o_ref[0, :, cols[pp]] = jnp.where(even, o[:t], o[t:]).astype(BF16)


def _fox_attention(h3, qa, ka):
    b, seq, _ = h3.shape
    t = ATT_Q
    return pl.pallas_call(
        _fox_kernel,
        grid=(b, seq // t),
        in_specs=[
            pl.BlockSpec((1, t, FOX_W), lambda i, j: (i, j, COL_FQ)),
            pl.BlockSpec((1, seq, FOX_W), lambda i, j: (i, 0, COL_FK)),
            pl.BlockSpec((1, seq, FOX_W), lambda i, j: (i, 0, COL_FV)),
            pl.BlockSpec((1, 2, t, FOX_W), lambda i, j: (i, 0, j, 0)),
            pl.BlockSpec((1, seq, FOX_W), lambda i, j: (i, 0, 0)),
        ],
        out_specs=pl.BlockSpec((1, t, FOX_W), lambda i, j: (i, j, 0)),
        out_shape=jax.ShapeDtypeStruct((b, seq, FOX_W), BF16),
        scratch_shapes=_attn_scratch(FOX_HEADS // 2, 2 * t, seq),
        compiler_params=_params("arbitrary", "arbitrary"),
        name="fox_attn",
    )(h3, h3, h3, qa, ka)


def _np_split3(x):
    hi = x.astype(BF16)
    r1 = x - hi.astype(np.float64)
    mid = r1.astype(BF16)
    lo = (r1 - mid.astype(np.float64)).astype(BF16)
    return hi, mid, lo


def _diff_alibi_columns(seq, slopes):
    scale = DIFF_QK_DIM ** -0.5
    pos = np.arange(seq)
    ka = np.zeros((seq, LANES), np.float32)
    ka[:, 0:3] = 1.0
    ka[:, 3:6] = (8 * (pos // 8))[:, None]
    ka[:, 6:9] = (pos % 8)[:, None]
    qa = np.zeros((DIFF_HEADS // 2, 4, seq, LANES), np.float32)
    for pp in range(DIFF_HEADS // 2):
        for mi in range(4):
            sl = float(slopes[2 * pp + mi // 2]) / scale
            a = _np_split3(-sl * pos.astype(np.float64))
            b = _np_split3(np.full(seq, sl, np.float64))
            for i in range(3):
                qa[pp, mi, :, i] = a[i].astype(np.float32)
                qa[pp, mi, :, 3 + i] = b[i].astype(np.float32)
                qa[pp, mi, :, 6 + i] = b[i].astype(np.float32)
    return jnp.asarray(qa, BF16), jnp.asarray(ka, BF16)


def _diff_kernel(q_ref, k_ref, v_ref, qa_ref, ka_ref, lamp_ref, g_ref, o_ref,
                 s_sc, mp_sc, acc_sc, *, lam_init):
    t = ATT_Q
    qi = pl.program_id(2)
    lane = lax.broadcasted_iota(jnp.int32, (1, LANES), 1)
    lamp = lamp_ref[...]
    a1 = jnp.sum(lamp[0:1] * lamp[1:2], axis=-1, keepdims=True)
    a2 = jnp.sum(lamp[2:3] * lamp[3:4], axis=-1, keepdims=True)
    lam = jnp.exp(a1) - jnp.exp(a2) + lam_init
    gain = g_ref[...]
    qp = q_ref[0].astype(F32)
    groups = []
    for mi in range(4):
        sel = (lane >= DIFF_QK_DIM * mi) & (lane < DIFF_QK_DIM * (mi + 1))
        groups.append(jnp.where(sel, qp, 0.0))
    maps = jnp.concatenate(groups, axis=0).astype(BF16)
    q_aug = [jnp.concatenate([maps, qa_ref[0].reshape(4 * t, LANES)], axis=1)]

    def key_tile(pp, k0):
        return jnp.concatenate([k_ref[0, pl.ds(k0, ATT_K), :], ka_ref[pl.ds(k0, ATT_K), :]],
                               axis=1)

    def value_tile(pp, k0):
        return v_ref[0, pl.ds(k0, ATT_K), :]

    (acc, l), = _causal_two_phase(q_aug, qi, key_tile, value_tile,
                                  DIFF_QK_DIM ** -0.5 * math.log2(math.e), s_sc, mp_sc, acc_sc)
    o = acc * (1.0 / l)
    outs = []
    for hh in range(2):
        oh = o[(2 * hh) * t:(2 * hh + 1) * t] - lam * o[(2 * hh + 1) * t:(2 * hh + 2) * t]
        half = (lane >= HEAD_DIM * hh) & (lane < HEAD_DIM * (hh + 1))
        ms = jnp.sum(jnp.where(half, oh * oh, 0.0), axis=-1, keepdims=True) * (1.0 / HEAD_DIM)
        outs.append(oh * lax.rsqrt(ms + LN_EPS) * gain * (1.0 - lam_init))
    o_ref[0] = jnp.where(lane < HEAD_DIM, outs[0], outs[1]).astype(BF16)


def _diff_attention(h3, lamp, gain_row, slopes, lam_init):
    b, seq, _ = h3.shape
    t = ATT_Q
    n_pairs = DIFF_HEADS // 2
    qa, ka = _diff_alibi_columns(seq, slopes)
    q0, k0, v0 = (c * (DIFF_W // LANES) for c in (COL_DQ, COL_DK, COL_DV))
    return pl.pallas_call(
        functools.partial(_diff_kernel, lam_init=lam_init),
        grid=(b, n_pairs, seq // t),
        in_specs=[
            pl.BlockSpec((1, t, LANES), lambda i, p, j: (i, j, q0 + p)),
            pl.BlockSpec((1, seq, LANES), lambda i, p, j: (i, 0, k0 + p)),
            pl.BlockSpec((1, seq, LANES), lambda i, p, j: (i, 0, v0 + p)),
            pl.BlockSpec((1, 4, t, LANES), lambda i, p, j: (p, 0, j, 0)),
            pl.BlockSpec((seq, LANES), lambda i, p, j: (0, 0)),
            pl.BlockSpec((8, LANES), lambda i, p, j: (0, 0)),
            pl.BlockSpec((1, LANES), lambda i, p, j: (0, 0)),
        ],
        out_specs=pl.BlockSpec((1, t, LANES), lambda i, p, j: (i, j, p)),
        out_shape=jax.ShapeDtypeStruct((b, seq, DIFF_W), BF16),
        scratch_shapes=_attn_scratch(1, 4 * t, seq),
        compiler_params=_params("arbitrary", "arbitrary", "arbitrary"),
        name="diff_attn",
    )(h3, h3, h3, qa, ka, lamp, gain_row)


def _dil_kernel(q_ref, k_ref, v_ref, slope_ref, o_ref,
                qf, kf, vf, kp, vp, m_st, l_st, a_st, *, seq):
    blk = DIL_BLOCK
    nblk = seq // blk
    lane = lax.broadcasted_iota(jnp.int32, (1, LANES), 1)
    even = lane < HEAD_DIM
    slope = slope_ref[0]
    nslope_e = -slope[:, 0:1]
    nslope_o = -slope[:, HEAD_DIM:HEAD_DIM + 1]
    qi = lax.broadcasted_iota(jnp.int32, (blk, 2 * blk), 0)
    ki = lax.broadcasted_iota(jnp.int32, (blk, 2 * blk), 1) - blk
    delta = qi - ki
    valid = (delta >= 0) & (delta <= blk)
    deltaf = delta.astype(F32)

    qf[...] = q_ref[0].astype(F32) * (HEAD_DIM ** -0.5)
    kf[...] = k_ref[0].astype(F32)
    vf[...] = v_ref[0].astype(F32)

    def block(rows, kw, vw, bias, merge):
        qb = qf[rows, :]
        q_stack = jnp.concatenate(
            [jnp.where(even, qb, 0.0), jnp.where(even, 0.0, qb)], axis=0).astype(BF16)
        s = _dot_nt(q_stack, kw) + bias
        m = jnp.max(s, axis=-1, keepdims=True)
        p = jnp.exp(s - m)
        l = jnp.sum(p, axis=-1, keepdims=True)
        acc = jnp.dot(p.astype(BF16), vw, preferred_element_type=F32)
        m_b = jnp.where(even, m[:blk], m[blk:])
        l_b = jnp.where(even, l[:blk], l[blk:])
        a_b = jnp.where(even, acc[:blk], acc[blk:])
        if merge:
            m_old = m_st[rows, :]
            m_new = jnp.maximum(m_old, m_b)
            w_old = jnp.exp(m_old - m_new)
            w_blk = jnp.exp(m_b - m_new)
            m_st[rows, :] = m_new
            l_st[rows, :] = w_old * l_st[rows, :] + w_blk * l_b
            a_st[rows, :] = w_old * a_st[rows, :] + w_blk * a_b
        else:
            m_st[rows, :] = m_b
            l_st[rows, :] = l_b
            a_st[rows, :] = a_b

    for _, dil in DIL_CONFIGS:
        per_res = nblk // dil
        length = seq // dil
        dist = deltaf * float(dil)
        bias = jnp.concatenate([jnp.where(valid, nslope_e * dist, NEG_INF),
                                jnp.where(valid, nslope_o * dist, NEG_INF)], axis=0)
        bias_first = bias[:, blk:]

        for b in range(nblk):
            res, n = divmod(b, per_res)
            win = slice((b - 1 if n else b) * blk, (b + 1) * blk)
            if dil == 1:
                block(pl.ds(b * blk, blk), k_ref[0, win, :], v_ref[0, win, :],
                      bias if n else bias_first, False)
                continue
            if n == 0:
                dst = slice(res * length, (res + 1) * length)
                kp[dst, :] = kf[pl.ds(res, length, stride=dil), :].astype(BF16)
                vp[dst, :] = vf[pl.ds(res, length, stride=dil), :].astype(BF16)
            block(pl.ds(dil * blk * n + res, blk, stride=dil), kp[win, :], vp[win, :],
                  bias if n else bias_first, True)

    o_ref[0] = (a_st[...] * (1.0 / l_st[...])).astype(BF16)


def _dil_attention(h3, slope_pairs):
    b, seq, _ = h3.shape
    npairs = DIL_HEADS // 2
    f32_buf = pltpu.VMEM((seq, LANES), F32)
    bf16_buf = pltpu.VMEM((seq, LANES), BF16)
    return pl.pallas_call(
        functools.partial(_dil_kernel, seq=seq),
        grid=(b, npairs),
        in_specs=[
            pl.BlockSpec((1, seq, LANES), lambda i, p: (i, 0, COL_CQ + p)),
            pl.BlockSpec((1, seq, LANES), lambda i, p: (i, 0, COL_CK + p)),
            pl.BlockSpec((1, seq, LANES), lambda i, p: (i, 0, COL_CV + p)),
            pl.BlockSpec((1, 1, LANES), lambda i, p: (p, 0, 0)),
        ],
        out_specs=pl.BlockSpec((1, seq, LANES), lambda i, p: (i, 0, p)),
        out_shape=jax.ShapeDtypeStruct((b, seq, DIL_W), BF16),
        scratch_shapes=[f32_buf, f32_buf, f32_buf, bf16_buf, bf16_buf,
                        f32_buf, f32_buf, f32_buf],
        compiler_params=_params("arbitrary", "arbitrary"),
        name="dil_attn",
    )(h3, h3, h3, slope_pairs)


def _memkv_kernel(m_ref, w_ref, o_ref):
    o_ref[0] = jnp.dot(m_ref[0].astype(BF16), w_ref[...],
                       preferred_element_type=F32).astype(BF16)


def _memkv(mem, w_kv):
    b, mt, _ = mem.shape
    return pl.pallas_call(
        _memkv_kernel,
        grid=(b,),
        in_specs=[pl.BlockSpec((1, mt, D_MODEL), lambda i: (i, 0, 0)),
                  pl.BlockSpec((D_MODEL, 2 * D_MODEL), lambda i: (0, 0))],
        out_specs=pl.BlockSpec((1, mt, 2 * D_MODEL), lambda i: (i, 0, 0)),
        out_shape=jax.ShapeDtypeStruct((b, mt, 2 * D_MODEL), BF16),
        compiler_params=_params("arbitrary"),
        name="memkv",
    )(mem, w_kv)


def _mix_mem_kernel(x_ref, fox_ref, diff_ref, dil_ref, wmix_ref, g1_ref, b1_ref,
                    kv_ref, wq_ref, wo_ref, g2_ref, b2_ref, o_ref, *, alpha):
    for r in range(0, MIX_ROW_TILE, MIX_ROWS):
        rows = slice(r, r + MIX_ROWS)
        mix = (jnp.dot(fox_ref[0, rows, :], wmix_ref[0:FOX_W, :], preferred_element_type=F32)
               + jnp.dot(diff_ref[0, rows, :], wmix_ref[FOX_W:FOX_W + DIFF_W, :],
                         preferred_element_type=F32)
               + jnp.dot(dil_ref[0, rows, :], wmix_ref[FOX_W + DIFF_W:, :],
                         preferred_element_type=F32))
        x1 = _layer_norm(alpha * x_ref[0, rows, :] + mix, g1_ref[...], b1_ref[...])
        q = jnp.dot(x1.astype(BF16), wq_ref[...], preferred_element_type=F32)
        q = (q * (MEM_HEAD_DIM ** -0.5)).astype(BF16)
        y = alpha * x1
        for h in range(MEM_HEADS):
            cols = slice(MEM_HEAD_DIM * h, MEM_HEAD_DIM * (h + 1))
            vcols = slice(D_MODEL + MEM_HEAD_DIM * h, D_MODEL + MEM_HEAD_DIM * (h + 1))
            s = _dot_nt(q[:, cols], kv_ref[0, :, cols])
            m = jnp.max(s, axis=-1, keepdims=True)
            p = jnp.exp(s - m)
            l = jnp.sum(p, axis=-1, keepdims=True)
            o = jnp.dot(p.astype(BF16), kv_ref[0, :, vcols],
                        preferred_element_type=F32) * (1.0 / l)
            y = y + jnp.dot(o.astype(BF16), wo_ref[cols, :], preferred_element_type=F32)
        o_ref[0, rows, :] = _layer_norm(y, g2_ref[...], b2_ref[...])


def _mix_mem_ln(x3, fox, diff, dil, w_mix, g1, b1, kv, w_q, w_o, g2, b2, alpha):
    bsz, seq, _ = x3.shape
    mt = kv.shape[1]
    row = lambda w: pl.BlockSpec((1, MIX_ROW_TILE, w), lambda i, j: (i, j, 0))
    const = lambda r, w: pl.BlockSpec((r, w), lambda i, j: (0, 0))
    return pl.pallas_call(
        functools.partial(_mix_mem_kernel, alpha=alpha),
        grid=(bsz, seq // MIX_ROW_TILE),
        in_specs=[
            row(D_MODEL), row(FOX_W), row(DIFF_W), row(DIL_W),
            const(MIX_W, D_MODEL), const(1, D_MODEL), const(1, D_MODEL),
            pl.BlockSpec((1, mt, 2 * D_MODEL), lambda i, j: (i, 0, 0)),
            const(D_MODEL, D_MODEL), const(D_MODEL, D_MODEL),
            const(1, D_MODEL), const(1, D_MODEL),
        ],
        out_specs=row(D_MODEL),
        out_shape=jax.ShapeDtypeStruct((bsz, seq, D_MODEL), F32),
        compiler_params=_params("arbitrary", "arbitrary"),
        name="mix_mem_ln",
    )(x3, fox, diff, dil, w_mix, g1, b1, kv, w_q, w_o, g2, b2)


def _ffn_kernel(x_ref, wup_ref, taps_ref, wdn_ref, g_ref, b_ref, o_ref,
                carry_ref, u_ref, h_ref, acc_ref, *, alpha):
    tm = FF_ROW_TILE
    n_slabs = 2 * FF_CHUNK // LANES

    @pl.when(pl.program_id(1) == 0)
    def _():
        carry_ref[...] = jnp.zeros_like(carry_ref)

    x = x_ref[0]
    xb = x.astype(BF16)
    acc_ref[...] = alpha * x

    def slab_cols(c, slab):
        part, half = divmod(slab, n_slabs // 2)
        lo = part * D_FF + c * FF_CHUNK + half * LANES
        return slice(lo, lo + LANES)

    def up_project(c):
        slot = c % 2
        for part in range(2):
            lo = part * D_FF + c * FF_CHUNK
            u = jnp.dot(xb, wup_ref[:, lo:lo + FF_CHUNK], preferred_element_type=F32)
            for half in range(n_slabs // 2):
                slab = part * (n_slabs // 2) + half
                piece = u[:, half * LANES:(half + 1) * LANES]
                u_ref[slot, slab, 0:8, :] = carry_ref[n_slabs * c + slab]
                u_ref[slot, slab, 8:8 + tm, :] = piece
                carry_ref[n_slabs * c + slab] = piece[tm - 8:tm, :]

    def conv_glu(c):
        slot = c % 2
        taps = [taps_ref[:, slab_cols(c, slab)] for slab in range(n_slabs)]
        for r in range(0, tm, FF_ROWS):
            ys = []
            for slab in range(n_slabs):
                t = taps[slab]
                ys.append(t[0:1] * u_ref[slot, slab, 6 + r:6 + r + FF_ROWS, :]
                          + t[1:2] * u_ref[slot, slab, 7 + r:7 + r + FF_ROWS, :]
                          + t[2:3] * u_ref[slot, slab, 8 + r:8 + r + FF_ROWS, :] + t[3:4])
            for half in range(n_slabs // 2):
                h = jax.nn.gelu(ys[half]) * ys[n_slabs // 2 + half]
                h_ref[slot, r:r + FF_ROWS, half * LANES:(half + 1) * LANES] = h.astype(BF16)

    def down_project(c):
        acc_ref[...] += jnp.dot(h_ref[c % 2], wdn_ref[c * FF_CHUNK:(c + 1) * FF_CHUNK, :],
                                preferred_element_type=F32)

    up_project(0)
    for c in range(N_FF_CHUNKS):
        if c + 1 < N_FF_CHUNKS:
            up_project(c + 1)
        conv_glu(c)
        down_project(c)
    o_ref[0] = _layer_norm(acc_ref[...], g_ref[...], b_ref[...])


def _ffn_ln(x3, w_up, taps, w_dn, g, b, alpha):
    bsz, seq, _ = x3.shape
    const = lambda r, w: pl.BlockSpec((r, w), lambda i, j: (0, 0))
    return pl.pallas_call(
        functools.partial(_ffn_kernel, alpha=alpha),
        grid=(bsz, seq // FF_ROW_TILE),
        in_specs=[
            pl.BlockSpec((1, FF_ROW_TILE, D_MODEL), lambda i, j: (i, j, 0)),
            const(D_MODEL, 2 * D_FF), const(8, 2 * D_FF), const(D_FF, D_MODEL),
            const(1, D_MODEL), const(1, D_MODEL),
        ],
        out_specs=pl.BlockSpec((1, FF_ROW_TILE, D_MODEL), lambda i, j: (i, j, 0)),
        out_shape=jax.ShapeDtypeStruct((bsz, seq, D_MODEL), F32),
        scratch_shapes=[pltpu.VMEM((2 * D_FF // LANES, 8, LANES), F32),
                        pltpu.VMEM((2, 2 * FF_CHUNK // LANES, 8 + FF_ROW_TILE, LANES), F32),
                        pltpu.VMEM((2, FF_ROW_TILE, FF_CHUNK), BF16),
                        pltpu.VMEM((FF_ROW_TILE, D_MODEL), F32)],
        compiler_params=_params("arbitrary", "arbitrary"),
        name="ffn_ln",
    )(x3, w_up, taps, w_dn, g, b)


def _pad_lanes(v):
    return jnp.pad(v.astype(F32), (0, LANES - v.shape[0])).reshape(1, LANES)


def kernel(x, mem, w_in, b_f, lambda_q1, lambda_k1, lambda_q2, lambda_k2, diff_norm_g, w_o,
           ln1_g, ln1_b, w_mq, w_mkv, w_mo, ln2_g, ln2_b, w_up, conv_w, conv_b, w_down,
           ln3_g, ln3_b):
    bsz, seq, _ = x.shape
    depth = w_in.shape[0]
    alpha = (2 * depth) ** 0.25
    diff_slopes, dil_slopes = _alibi_slopes()
    slope_pairs = jnp.asarray(
        np.repeat(dil_slopes.reshape(DIL_HEADS // 2, 2), HEAD_DIM, axis=1)
        .reshape(DIL_HEADS // 2, 1, LANES))
    row2 = lambda v: v.astype(F32).reshape(1, D_MODEL)

    x2 = x.reshape(bsz * seq, D_MODEL)
    for l in range(depth):
        lam_init = 0.8 - 0.6 * math.exp(-0.3 * l)
        w_main, w_ff = _inproj_weights(w_in, l)
        h2, f2 = _inproj(x2, w_main, w_ff)
        h3 = h2.reshape(bsz, seq, MAIN_COLS)
        fox_qa, fox_ka = _fox_prep(f2.reshape(bsz, seq, LANES), _pad_lanes(b_f[l]))
        fox = _fox_attention(h3, fox_qa, fox_ka)
        lamp = jnp.zeros((8, LANES), F32)
        for i, v in enumerate((lambda_q1, lambda_k1, lambda_q2, lambda_k2)):
            lamp = lamp.at[i, :DIFF_QK_DIM].set(v[l].astype(F32))
        gain_row = jnp.tile(diff_norm_g[l].astype(F32), 2).reshape(1, LANES)
        diff = _diff_attention(h3, lamp, gain_row, diff_slopes, lam_init)
        dil = _dil_attention(h3, slope_pairs)
        kv = _memkv(mem, w_mkv[l].astype(BF16))
        x3 = _mix_mem_ln(x2.reshape(bsz, seq, D_MODEL), fox, diff, dil, w_o[l].astype(BF16),
                         row2(ln1_g[l]), row2(ln1_b[l]), kv, w_mq[l].astype(BF16),
                         w_mo[l].astype(BF16), row2(ln2_g[l]), row2(ln2_b[l]), alpha)
        taps = jnp.concatenate([conv_w[l][:, 0, :], conv_b[l][None, :],
                                jnp.zeros((4, 2 * D_FF), F32)], axis=0).astype(F32)
        x3 = _ffn_ln(x3, w_up[l].astype(BF16), taps, w_down[l].astype(BF16),
                     row2(ln3_g[l]), row2(ln3_b[l]), alpha)
        x2 = x3.reshape(bsz * seq, D_MODEL)
    return x2.reshape(bsz, seq, D_MODEL)
```

```python
import functools
import math

import numpy as np
import jax
import jax.numpy as jnp
from jax import lax
from jax.experimental import pallas as pl
from jax.experimental.pallas import tpu as pltpu

F32 = jnp.float32
BF16 = jnp.bfloat16

D_MODEL = 1024
HEAD_DIM = 64
FOX_HEADS = 4
DIFF_HEADS = 4
DIL_HEADS = 8
FOX_W = FOX_HEADS * HEAD_DIM
DIFF_W = DIFF_HEADS * HEAD_DIM
DIL_W = DIL_HEADS * HEAD_DIM
MIX_W = FOX_W + DIFF_W + DIL_W
DIFF_QK_DIM = HEAD_DIM // 2
DIL_CONFIGS = ((128, 1), (512, 4), (2048, 16))
DIL_BLOCK = 128
MEM_HEADS = 4
MEM_HEAD_DIM = D_MODEL // MEM_HEADS
D_FF = 2816
LN_EPS = 1e-5
NEG_INF = -1e30

LANES = 128
VMEM_LIMIT = 56 * 1024 * 1024

ROW_TILE = 512
FF_ROW_TILE = 512
MIX_ROW_TILE = 1024
MIX_ROWS = 1024
IN_CHUNK = 512
ATT_Q = 512
ATT_K = 256
ATT_GROUP = 4
FF_CHUNK = 256
N_FF_CHUNKS = D_FF // FF_CHUNK
FF_ROWS = 64
PREP_BLOCK = 256

COL_FQ, COL_FK, COL_FV, COL_DQ, COL_DK, COL_DV = 0, 1, 2, 3, 4, 5
COL_CQ, COL_CK, COL_CV = 12, 16, 20
MAIN_COLS = 3 * MIX_W


def _alibi_slopes():
    n = DIFF_HEADS + DIL_HEADS
    slopes = 2.0 ** (-8.0 * np.arange(1, n + 1) / n)
    stride = n // DIFF_HEADS
    diff_sel = np.arange(0, n, stride)[:DIFF_HEADS]
    dil_sel = np.setdiff1d(np.arange(n), diff_sel)
    return (np.asarray(slopes[diff_sel], np.float32), np.asarray(slopes[dil_sel], np.float32))


def _params(*sem):
    return pltpu.CompilerParams(dimension_semantics=sem, vmem_limit_bytes=VMEM_LIMIT)


def _layer_norm(y, g, b):
    mu = jnp.mean(y, axis=-1, keepdims=True)
    d = y - mu
    var = jnp.mean(d * d, axis=-1, keepdims=True)
    return d * lax.rsqrt(var + LN_EPS) * g + b


def _dot_nt(a, b):
    return lax.dot_general(a, b, (((1,), (1,)), ((), ())), preferred_element_type=F32)


def _inproj_weights_kernel(w_ref, main_ref, ff_ref):
    ff_lo, ff_hi = 3 * FOX_W, 3 * FOX_W + FOX_HEADS
    w = w_ref[0]
    main_ref[:, :ff_lo] = w[:, :ff_lo].astype(BF16)
    main_ref[:, ff_lo:] = w[:, ff_hi:].astype(BF16)
    ff = jnp.concatenate([w[:, ff_lo:ff_hi], jnp.zeros((w.shape[0], LANES - FOX_HEADS), F32)],
                         axis=1)
    ff_ref[...] = ff.astype(BF16)


def _inproj_weights(w_in, layer):
    rows = 256
    cols = w_in.shape[2]
    return pl.pallas_call(
        _inproj_weights_kernel,
        grid=(D_MODEL // rows,),
        in_specs=[pl.BlockSpec((1, rows, cols), lambda i: (layer, i, 0))],
        out_specs=[pl.BlockSpec((rows, MAIN_COLS), lambda i: (i, 0)),
                   pl.BlockSpec((rows, LANES), lambda i: (i, 0))],
        out_shape=[jax.ShapeDtypeStruct((D_MODEL, MAIN_COLS), BF16),
                   jax.ShapeDtypeStruct((D_MODEL, LANES), BF16)],
        compiler_params=_params("arbitrary"),
        name="inproj_weights",
    )(w_in)


def _inproj_kernel(x_ref, w_ref, wf_ref, h_ref, f_ref):
    xb = x_ref[...].astype(BF16)
    for n in range(0, MAIN_COLS, IN_CHUNK):
        h_ref[:, n:n + IN_CHUNK] = jnp.dot(
            xb, w_ref[:, n:n + IN_CHUNK], preferred_element_type=F32).astype(BF16)
    f_ref[...] = jnp.dot(xb, wf_ref[...], preferred_element_type=F32)


def _inproj(x2, w_main, w_ff):
    m = x2.shape[0]
    return pl.pallas_call(
        _inproj_kernel,
        grid=(m // ROW_TILE,),
        in_specs=[
            pl.BlockSpec((ROW_TILE, D_MODEL), lambda i: (i, 0)),
            pl.BlockSpec((D_MODEL, MAIN_COLS), lambda i: (0, 0)),
            pl.BlockSpec((D_MODEL, LANES), lambda i: (0, 0)),
        ],
        out_specs=[
            pl.BlockSpec((ROW_TILE, MAIN_COLS), lambda i: (i, 0)),
            pl.BlockSpec((ROW_TILE, LANES), lambda i: (i, 0)),
        ],
        out_shape=[
            jax.ShapeDtypeStruct((m, MAIN_COLS), BF16),
            jax.ShapeDtypeStruct((m, LANES), F32),
        ],
        compiler_params=_params("arbitrary"),
        name="inproj",
    )(x2, w_main, w_ff)


def _split3(x):
    hi = x.astype(BF16).astype(F32)
    r1 = x - hi
    mid = r1.astype(BF16).astype(F32)
    lo = (r1 - mid).astype(BF16).astype(F32)
    return hi, mid, lo


def _fox_bias_placement():
    width = LANES * (FOX_HEADS // 2)
    place = np.zeros((3, LANES, width), np.float32)
    ones = np.zeros((8, width), np.float32)
    for pp in range(FOX_HEADS // 2):
        base = LANES * pp
        for i in range(3):
            place[0, 8 * i + 2 * pp, base + i] = 1.0
            place[1, 8 * i + 2 * pp + 1, base + 6 + i] = 1.0
            place[2, 8 * i + 2 * pp, base + 3 + i] = -1.0
            place[2, 8 * i + 2 * pp + 1, base + 9 + i] = -1.0
        ones[0, base + 3:base + 6] = 1.0
        ones[1, base + 9:base + 12] = 1.0
        ones[2, base:base + 3] = 1.0
        ones[2, base + 6:base + 9] = 1.0
    return jnp.asarray(place, BF16), jnp.asarray(ones, F32)


def _fox_prep_kernel(f_ref, bf_ref, place_ref, ones_ref, qa_ref, ka_ref, *, seq):
    r = lax.broadcasted_iota(jnp.int32, (PREP_BLOCK, PREP_BLOCK), 0)
    c = lax.broadcasted_iota(jnp.int32, (PREP_BLOCK, PREP_BLOCK), 1)
    upper = jnp.where(r <= c, 1.0, 0.0).astype(BF16)
    pad = jnp.zeros((LANES - 3 * 8, PREP_BLOCK), F32)
    offset = jnp.zeros((8, 1), F32)
    for blk in range(seq // PREP_BLOCK):
        rows = slice(blk * PREP_BLOCK, (blk + 1) * PREP_BLOCK)
        x = (f_ref[0, rows, :] + bf_ref[...]).T[0:8, :]
        logf = jnp.minimum(x, 0.0) - jnp.log1p(jnp.exp(-jnp.abs(x)))
        cs = sum(jnp.dot(piece.astype(BF16), upper, preferred_element_type=F32)
                 for piece in _split3(logf))
        cblk = cs + offset
        offset = cblk[:, PREP_BLOCK - 1:PREP_BLOCK]
        pieces = jnp.concatenate(_split3(cblk) + (pad,), axis=0).T.astype(BF16)
        placed = [jnp.dot(pieces, place_ref[i], preferred_element_type=F32) + ones_ref[i:i + 1, :]
                  for i in range(3)]
        qa_ref[0, 0, rows, :] = placed[0].astype(BF16)
        qa_ref[0, 1, rows, :] = placed[1].astype(BF16)
        ka_ref[0, rows, :] = placed[2].astype(BF16)


def _fox_prep(f3, bf_row):
    b, seq, _ = f3.shape
    width = LANES * (FOX_HEADS // 2)
    place, ones = _fox_bias_placement()
    return pl.pallas_call(
        functools.partial(_fox_prep_kernel, seq=seq),
        grid=(b,),
        in_specs=[
            pl.BlockSpec((1, seq, LANES), lambda i: (i, 0, 0)),
            pl.BlockSpec((1, LANES), lambda i: (0, 0)),
            pl.BlockSpec((3, LANES, width), lambda i: (0, 0, 0)),
            pl.BlockSpec((8, width), lambda i: (0, 0)),
        ],
        out_specs=[
            pl.BlockSpec((1, 2, seq, width), lambda i: (i, 0, 0, 0)),
            pl.BlockSpec((1, seq, width), lambda i: (i, 0, 0)),
        ],
        out_shape=[
            jax.ShapeDtypeStruct((b, 2, seq, width), BF16),
            jax.ShapeDtypeStruct((b, seq, width), BF16),
        ],
        compiler_params=_params("arbitrary"),
        name="fox_prep",
    )(f3, bf_row, place, ones)


def _causal_two_phase(q_aug, qi, key_tile, value_tile, log2_scale, s_sc, mp_sc, acc_sc):
    tq, tk = ATT_Q, ATT_K
    per_step = 2
    n_diag = tq // tk
    assert n_diag == per_step
    n_pairs = len(q_aug)
    rows = q_aug[0].shape[0]
    r = lax.broadcasted_iota(jnp.int32, (rows, tk), 0) & (tq - 1)
    c = lax.broadcasted_iota(jnp.int32, (rows, tk), 1)
    ones = jnp.ones((tk, LANES), BF16)
    mp_sc[...] = jnp.full(mp_sc.shape, NEG_INF, F32)

    def phase1(first_tile, count, diagonal):
        for pp in range(n_pairs):
            part = mp_sc[pp]
            for i in range(count):
                k0 = pl.multiple_of((first_tile + i) * tk, tk)
                s = _dot_nt(q_aug[pp], key_tile(pp, k0)) * log2_scale
                if diagonal:
                    s = jnp.where(c + i * tk <= r, s, NEG_INF)
                s_sc[pp, :, pl.ds(k0, tk)] = s
                part = jnp.maximum(part, jnp.maximum(s[:, :LANES], s[:, LANES:]))
            mp_sc[pp] = part

    def phase2(first_tile, count):
        for pp in range(n_pairs):
            pv = None
            for i in range(count):
                k0 = pl.multiple_of((first_tile + i) * tk, tk)
                p = jnp.exp2(s_sc[pp, :, pl.ds(k0, tk)] - row_max[pp])
                d = jnp.dot(p.astype(BF16), jnp.concatenate([value_tile(pp, k0), ones], axis=1),
                            preferred_element_type=F32)
                pv = d if pv is None else pv + d
            acc_sc[pp] += pv

    def run(phase, n_tiles):
        long_steps = n_tiles // ATT_GROUP

        def long_body(g, carry):
            phase(g * ATT_GROUP, ATT_GROUP)
            return carry

        def short_body(g, carry):
            phase(long_steps * ATT_GROUP + g * per_step, per_step)
            return carry

        lax.fori_loop(0, long_steps, long_body, 0)
        lax.fori_loop(0, (n_tiles - long_steps * ATT_GROUP) // per_step, short_body, 0)

    n_full = qi * n_diag
    run(functools.partial(phase1, diagonal=False), n_full)
    phase1(n_full, n_diag, True)

    acc_sc[...] = jnp.zeros(acc_sc.shape, F32)
    row_max = [jnp.max(mp_sc[pp], axis=-1, keepdims=True) for pp in range(n_pairs)]
    run(phase2, n_full + n_diag)
    return [(acc_sc[pp, :, :LANES], acc_sc[pp, :, LANES:LANES + 1]) for pp in range(n_pairs)]


def _attn_scratch(n_pairs, rows, seq):
    return [pltpu.VMEM((n_pairs, rows, seq), F32), pltpu.VMEM((n_pairs, rows, LANES), F32),
            pltpu.VMEM((n_pairs, rows, 2 * LANES), F32)]


def _fox_kernel(q_ref, k_ref, v_ref, qa_ref, ka_ref, o_ref, s_sc, mp_sc, acc_sc):
    t = ATT_Q
    n_pairs = FOX_HEADS // 2
    qi = pl.program_id(1)
    lane = lax.broadcasted_iota(jnp.int32, (1, LANES), 1)
    even = lane < HEAD_DIM
    cols = [slice(LANES * pp, LANES * (pp + 1)) for pp in range(n_pairs)]
    q_aug = []
    for pp in range(n_pairs):
        qp = q_ref[0, :, cols[pp]].astype(F32) * (HEAD_DIM ** -0.5)
        heads = jnp.concatenate(
            [jnp.where(even, qp, 0.0), jnp.where(even, 0.0, qp)], axis=0).astype(BF16)
        bias = jnp.concatenate([qa_ref[0, 0, :, cols[pp]], qa_ref[0, 1, :, cols[pp]]], axis=0)
        q_aug.append(jnp.concatenate([heads, bias], axis=1))

    def key_tile(pp, k0):
        return jnp.concatenate([k_ref[0, pl.ds(k0, ATT_K), cols[pp]],
                                ka_ref[0, pl.ds(k0, ATT_K), cols[pp]]], axis=1)

    def value_tile(pp, k0):
        return v_ref[0, pl.ds(k0, ATT_K), cols[pp]]

    res = _causal_two_phase(q_aug, qi, key_tile, value_tile, math.log2(math.e),
                            s_sc, mp_sc, acc_sc)
    for pp in range(n_pairs):
        acc, l = res[pp]
        o = acc * (1.0 / l)
        o_ref[0, :, cols[pp]] = jnp.where(even, o[:t], o[t:]).astype(BF16)


def _fox_attention(h3, qa, ka):
    b, seq, _ = h3.shape
    t = ATT_Q
    return pl.pallas_call(
        _fox_kernel,
        grid=(b, seq // t),
        in_specs=[
            pl.BlockSpec((1, t, FOX_W), lambda i, j: (i, j, COL_FQ)),
            pl.BlockSpec((1, seq, FOX_W), lambda i, j: (i, 0, COL_FK)),
            pl.BlockSpec((1, seq, FOX_W), lambda i, j: (i, 0, COL_FV)),
            pl.BlockSpec((1, 2, t, FOX_W), lambda i, j: (i, 0, j, 0)),
            pl.BlockSpec((1, seq, FOX_W), lambda i, j: (i, 0, 0)),
        ],
        out_specs=pl.BlockSpec((1, t, FOX_W), lambda i, j: (i, j, 0)),
        out_shape=jax.ShapeDtypeStruct((b, seq, FOX_W), BF16),
        scratch_shapes=_attn_scratch(FOX_HEADS // 2, 2 * t, seq),
        compiler_params=_params("arbitrary", "arbitrary"),
        name="fox_attn",
    )(h3, h3, h3, qa, ka)


def _np_split3(x):
    hi = x.astype(BF16)
    r1 = x - hi.astype(np.float64)
    mid = r1.astype(BF16)
    lo = (r1 - mid.astype(np.float64)).astype(BF16)
    return hi, mid, lo


def _diff_alibi_columns(seq, slopes):
    scale = DIFF_QK_DIM ** -0.5
    pos = np.arange(seq)
    ka = np.zeros((seq, LANES), np.float32)
    ka[:, 0:3] = 1.0
    ka[:, 3:6] = (8 * (pos // 8))[:, None]
    ka[:, 6:9] = (pos % 8)[:, None]
    qa = np.zeros((DIFF_HEADS // 2, 4, seq, LANES), np.float32)
    for pp in range(DIFF_HEADS // 2):
        for mi in range(4):
            sl = float(slopes[2 * pp + mi // 2]) / scale
            a = _np_split3(-sl * pos.astype(np.float64))
            b = _np_split3(np.full(seq, sl, np.float64))
            for i in range(3):
                qa[pp, mi, :, i] = a[i].astype(np.float32)
                qa[pp, mi, :, 3 + i] = b[i].astype(np.float32)
                qa[pp, mi, :, 6 + i] = b[i].astype(np.float32)
    return jnp.asarray(qa, BF16), jnp.asarray(ka, BF16)


def _diff_kernel(q_ref, k_ref, v_ref, qa_ref, ka_ref, lamp_ref, g_ref, o_ref,
                 s_sc, mp_sc, acc_sc, *, lam_init):
    t = ATT_Q
    qi = pl.program_id(2)
    lane = lax.broadcasted_iota(jnp.int32, (1, LANES), 1)
    lamp = lamp_ref[...]
    a1 = jnp.sum(lamp[0:1] * lamp[1:2], axis=-1, keepdims=True)
    a2 = jnp.sum(lamp[2:3] * lamp[3:4], axis=-1, keepdims=True)
    lam = jnp.exp(a1) - jnp.exp(a2) + lam_init
    gain = g_ref[...]
    qp = q_ref[0].astype(F32)
    groups = []
    for mi in range(4):
        sel = (lane >= DIFF_QK_DIM * mi) & (lane < DIFF_QK_DIM * (mi + 1))
        groups.append(jnp.where(sel, qp, 0.0))
    maps = jnp.concatenate(groups, axis=0).astype(BF16)
    q_aug = [jnp.concatenate([maps, qa_ref[0].reshape(4 * t, LANES)], axis=1)]

    def key_tile(pp, k0):
        return jnp.concatenate([k_ref[0, pl.ds(k0, ATT_K), :], ka_ref[pl.ds(k0, ATT_K), :]],
                               axis=1)

    def value_tile(pp, k0):
        return v_ref[0, pl.ds(k0, ATT_K), :]

    (acc, l), = _causal_two_phase(q_aug, qi, key_tile, value_tile,
                                  DIFF_QK_DIM ** -0.5 * math.log2(math.e), s_sc, mp_sc, acc_sc)
    o = acc * (1.0 / l)
    outs = []
    for hh in range(2):
        oh = o[(2 * hh) * t:(2 * hh + 1) * t] - lam * o[(2 * hh + 1) * t:(2 * hh + 2) * t]
        half = (lane >= HEAD_DIM * hh) & (lane < HEAD_DIM * (hh + 1))
        ms = jnp.sum(jnp.where(half, oh * oh, 0.0), axis=-1, keepdims=True) * (1.0 / HEAD_DIM)
        outs.append(oh * lax.rsqrt(ms + LN_EPS) * gain * (1.0 - lam_init))
    o_ref[0] = jnp.where(lane < HEAD_DIM, outs[0], outs[1]).astype(BF16)


def _diff_attention(h3, lamp, gain_row, slopes, lam_init):
    b, seq, _ = h3.shape
    t = ATT_Q
    n_pairs = DIFF_HEADS // 2
    qa, ka = _diff_alibi_columns(seq, slopes)
    q0, k0, v0 = (c * (DIFF_W // LANES) for c in (COL_DQ, COL_DK, COL_DV))
    return pl.pallas_call(
        functools.partial(_diff_kernel, lam_init=lam_init),
        grid=(b, n_pairs, seq // t),
        in_specs=[
            pl.BlockSpec((1, t, LANES), lambda i, p, j: (i, j, q0 + p)),
            pl.BlockSpec((1, seq, LANES), lambda i, p, j: (i, 0, k0 + p)),
            pl.BlockSpec((1, seq, LANES), lambda i, p, j: (i, 0, v0 + p)),
            pl.BlockSpec((1, 4, t, LANES), lambda i, p, j: (p, 0, j, 0)),
            pl.BlockSpec((seq, LANES), lambda i, p, j: (0, 0)),
            pl.BlockSpec((8, LANES), lambda i, p, j: (0, 0)),
            pl.BlockSpec((1, LANES), lambda i, p, j: (0, 0)),
        ],
        out_specs=pl.BlockSpec((1, t, LANES), lambda i, p, j: (i, j, p)),
        out_shape=jax.ShapeDtypeStruct((b, seq, DIFF_W), BF16),
        scratch_shapes=_attn_scratch(1, 4 * t, seq),
        compiler_params=_params("arbitrary", "arbitrary", "arbitrary"),
        name="diff_attn",
    )(h3, h3, h3, qa, ka, lamp, gain_row)


def _dil_kernel(q_ref, k_ref, v_ref, slope_ref, o_ref,
                qf, kf, vf, kp, vp, m_st, l_st, a_st, *, seq):
    blk = DIL_BLOCK
    nblk = seq // blk
    lane = lax.broadcasted_iota(jnp.int32, (1, LANES), 1)
    even = lane < HEAD_DIM
    slope = slope_ref[0]
    nslope_e = -slope[:, 0:1]
    nslope_o = -slope[:, HEAD_DIM:HEAD_DIM + 1]
    qi = lax.broadcasted_iota(jnp.int32, (blk, 2 * blk), 0)
    ki = lax.broadcasted_iota(jnp.int32, (blk, 2 * blk), 1) - blk
    delta = qi - ki
    valid = (delta >= 0) & (delta <= blk)
    deltaf = delta.astype(F32)

    qf[...] = q_ref[0].astype(F32) * (HEAD_DIM ** -0.5)
    kf[...] = k_ref[0].astype(F32)
    vf[...] = v_ref[0].astype(F32)

    def block(rows, kw, vw, bias, merge):
        qb = qf[rows, :]
        q_stack = jnp.concatenate(
            [jnp.where(even, qb, 0.0), jnp.where(even, 0.0, qb)], axis=0).astype(BF16)
        s = _dot_nt(q_stack, kw) + bias
        m = jnp.max(s, axis=-1, keepdims=True)
        p = jnp.exp(s - m)
        l = jnp.sum(p, axis=-1, keepdims=True)
        acc = jnp.dot(p.astype(BF16), vw, preferred_element_type=F32)
        m_b = jnp.where(even, m[:blk], m[blk:])
        l_b = jnp.where(even, l[:blk], l[blk:])
        a_b = jnp.where(even, acc[:blk], acc[blk:])
        if merge:
            m_old = m_st[rows, :]
            m_new = jnp.maximum(m_old, m_b)
            w_old = jnp.exp(m_old - m_new)
            w_blk = jnp.exp(m_b - m_new)
            m_st[rows, :] = m_new
            l_st[rows, :] = w_old * l_st[rows, :] + w_blk * l_b
            a_st[rows, :] = w_old * a_st[rows, :] + w_blk * a_b
        else:
            m_st[rows, :] = m_b
            l_st[rows, :] = l_b
            a_st[rows, :] = a_b

    for _, dil in DIL_CONFIGS:
        per_res = nblk // dil
        length = seq // dil
        dist = deltaf * float(dil)
        bias = jnp.concatenate([jnp.where(valid, nslope_e * dist, NEG_INF),
                                jnp.where(valid, nslope_o * dist, NEG_INF)], axis=0)
        bias_first = bias[:, blk:]

        for b in range(nblk):
            res, n = divmod(b, per_res)
            win = slice((b - 1 if n else b) * blk, (b + 1) * blk)
            if dil == 1:
                block(pl.ds(b * blk, blk), k_ref[0, win, :], v_ref[0, win, :],
                      bias if n else bias_first, False)
                continue
            if n == 0:
                dst = slice(res * length, (res + 1) * length)
                kp[dst, :] = kf[pl.ds(res, length, stride=dil), :].astype(BF16)
                vp[dst, :] = vf[pl.ds(res, length, stride=dil), :].astype(BF16)
            block(pl.ds(dil * blk * n + res, blk, stride=dil), kp[win, :], vp[win, :],
                  bias if n else bias_first, True)

    o_ref[0] = (a_st[...] * (1.0 / l_st[...])).astype(BF16)


def _dil_attention(h3, slope_pairs):
    b, seq, _ = h3.shape
    npairs = DIL_HEADS // 2
    f32_buf = pltpu.VMEM((seq, LANES), F32)
    bf16_buf = pltpu.VMEM((seq, LANES), BF16)
    return pl.pallas_call(
        functools.partial(_dil_kernel, seq=seq),
        grid=(b, npairs),
        in_specs=[
            pl.BlockSpec((1, seq, LANES), lambda i, p: (i, 0, COL_CQ + p)),
            pl.BlockSpec((1, seq, LANES), lambda i, p: (i, 0, COL_CK + p)),
            pl.BlockSpec((1, seq, LANES), lambda i, p: (i, 0, COL_CV + p)),
            pl.BlockSpec((1, 1, LANES), lambda i, p: (p, 0, 0)),
        ],
        out_specs=pl.BlockSpec((1, seq, LANES), lambda i, p: (i, 0, p)),
        out_shape=jax.ShapeDtypeStruct((b, seq, DIL_W), BF16),
        scratch_shapes=[f32_buf, f32_buf, f32_buf, bf16_buf, bf16_buf,
                        f32_buf, f32_buf, f32_buf],
        compiler_params=_params("arbitrary", "arbitrary"),
        name="dil_attn",
    )(h3, h3, h3, slope_pairs)


def _memkv_kernel(m_ref, w_ref, o_ref):
    o_ref[0] = jnp.dot(m_ref[0].astype(BF16), w_ref[...],
                       preferred_element_type=F32).astype(BF16)


def _memkv(mem, w_kv):
    b, mt, _ = mem.shape
    return pl.pallas_call(
        _memkv_kernel,
        grid=(b,),
        in_specs=[pl.BlockSpec((1, mt, D_MODEL), lambda i: (i, 0, 0)),
                  pl.BlockSpec((D_MODEL, 2 * D_MODEL), lambda i: (0, 0))],
        out_specs=pl.BlockSpec((1, mt, 2 * D_MODEL), lambda i: (i, 0, 0)),
        out_shape=jax.ShapeDtypeStruct((b, mt, 2 * D_MODEL), BF16),
        compiler_params=_params("arbitrary"),
        name="memkv",
    )(mem, w_kv)


def _mix_mem_kernel(x_ref, fox_ref, diff_ref, dil_ref, wmix_ref, g1_ref, b1_ref,
                    kv_ref, wq_ref, wo_ref, g2_ref, b2_ref, o_ref, *, alpha):
    for r in range(0, MIX_ROW_TILE, MIX_ROWS):
        rows = slice(r, r + MIX_ROWS)
        mix = (jnp.dot(fox_ref[0, rows, :], wmix_ref[0:FOX_W, :], preferred_element_type=F32)
               + jnp.dot(diff_ref[0, rows, :], wmix_ref[FOX_W:FOX_W + DIFF_W, :],
                         preferred_element_type=F32)
               + jnp.dot(dil_ref[0, rows, :], wmix_ref[FOX_W + DIFF_W:, :],
                         preferred_element_type=F32))
        x1 = _layer_norm(alpha * x_ref[0, rows, :] + mix, g1_ref[...], b1_ref[...])
        q = jnp.dot(x1.astype(BF16), wq_ref[...], preferred_element_type=F32)
        q = (q * (MEM_HEAD_DIM ** -0.5)).astype(BF16)
        y = alpha * x1
        for h in range(MEM_HEADS):
            cols = slice(MEM_HEAD_DIM * h, MEM_HEAD_DIM * (h + 1))
            vcols = slice(D_MODEL + MEM_HEAD_DIM * h, D_MODEL + MEM_HEAD_DIM * (h + 1))
            s = _dot_nt(q[:, cols], kv_ref[0, :, cols])
            m = jnp.max(s, axis=-1, keepdims=True)
            p = jnp.exp(s - m)
            l = jnp.sum(p, axis=-1, keepdims=True)
            o = jnp.dot(p.astype(BF16), kv_ref[0, :, vcols],
                        preferred_element_type=F32) * (1.0 / l)
            y = y + jnp.dot(o.astype(BF16), wo_ref[cols, :], preferred_element_type=F32)
        o_ref[0, rows, :] = _layer_norm(y, g2_ref[...], b2_ref[...])


def _mix_mem_ln(x3, fox, diff, dil, w_mix, g1, b1, kv, w_q, w_o, g2, b2, alpha):
    bsz, seq, _ = x3.shape
    mt = kv.shape[1]
    row = lambda w: pl.BlockSpec((1, MIX_ROW_TILE, w), lambda i, j: (i, j, 0))
    const = lambda r, w: pl.BlockSpec((r, w), lambda i, j: (0, 0))
    return pl.pallas_call(
        functools.partial(_mix_mem_kernel, alpha=alpha),
        grid=(bsz, seq // MIX_ROW_TILE),
        in_specs=[
            row(D_MODEL), row(FOX_W), row(DIFF_W), row(DIL_W),
            const(MIX_W, D_MODEL), const(1, D_MODEL), const(1, D_MODEL),
            pl.BlockSpec((1, mt, 2 * D_MODEL), lambda i, j: (i, 0, 0)),
            const(D_MODEL, D_MODEL), const(D_MODEL, D_MODEL),
            const(1, D_MODEL), const(1, D_MODEL),
        ],
        out_specs=row(D_MODEL),
        out_shape=jax.ShapeDtypeStruct((bsz, seq, D_MODEL), F32),
        compiler_params=_params("arbitrary", "arbitrary"),
        name="mix_mem_ln",
    )(x3, fox, diff, dil, w_mix, g1, b1, kv, w_q, w_o, g2, b2)


def _ffn_kernel(x_ref, wup_ref, taps_ref, wdn_ref, g_ref, b_ref, o_ref,
                carry_ref, u_ref, h_ref, acc_ref, *, alpha):
    tm = FF_ROW_TILE
    n_slabs = 2 * FF_CHUNK // LANES

    @pl.when(pl.program_id(1) == 0)
    def _():
        carry_ref[...] = jnp.zeros_like(carry_ref)

    x = x_ref[0]
    xb = x.astype(BF16)
    acc_ref[...] = alpha * x

    def slab_cols(c, slab):
        part, half = divmod(slab, n_slabs // 2)
        lo = part * D_FF + c * FF_CHUNK + half * LANES
        return slice(lo, lo + LANES)

    def up_project(c):
        slot = c % 2
        for part in range(2):
            lo = part * D_FF + c * FF_CHUNK
            u = jnp.dot(xb, wup_ref[:, lo:lo + FF_CHUNK], preferred_element_type=F32)
            for half in range(n_slabs // 2):
                slab = part * (n_slabs // 2) + half
                piece = u[:, half * LANES:(half + 1) * LANES]
                u_ref[slot, slab, 0:8, :] = carry_ref[n_slabs * c + slab]
                u_ref[slot, slab, 8:8 + tm, :] = piece
                carry_ref[n_slabs * c + slab] = piece[tm - 8:tm, :]

    def conv_glu(c):
        slot = c % 2
        taps = [taps_ref[:, slab_cols(c, slab)] for slab in range(n_slabs)]
        for r in range(0, tm, FF_ROWS):
            ys = []
            for slab in range(n_slabs):
                t = taps[slab]
                ys.append(t[0:1] * u_ref[slot, slab, 6 + r:6 + r + FF_ROWS, :]
                          + t[1:2] * u_ref[slot, slab, 7 + r:7 + r + FF_ROWS, :]
                          + t[2:3] * u_ref[slot, slab, 8 + r:8 + r + FF_ROWS, :] + t[3:4])
            for half in range(n_slabs // 2):
                h = jax.nn.gelu(ys[half]) * ys[n_slabs // 2 + half]
                h_ref[slot, r:r + FF_ROWS, half * LANES:(half + 1) * LANES] = h.astype(BF16)

    def down_project(c):
        acc_ref[...] += jnp.dot(h_ref[c % 2], wdn_ref[c * FF_CHUNK:(c + 1) * FF_CHUNK, :],
                                preferred_element_type=F32)

    up_project(0)
    for c in range(N_FF_CHUNKS):
        if c + 1 < N_FF_CHUNKS:
            up_project(c + 1)
        conv_glu(c)
        down_project(c)
    o_ref[0] = _layer_norm(acc_ref[...], g_ref[...], b_ref[...])


def _ffn_ln(x3, w_up, taps, w_dn, g, b, alpha):
    bsz, seq, _ = x3.shape
    const = lambda r, w: pl.BlockSpec((r, w), lambda i, j: (0, 0))
    return pl.pallas_call(
        functools.partial(_ffn_kernel, alpha=alpha),
        grid=(bsz, seq // FF_ROW_TILE),
        in_specs=[
            pl.BlockSpec((1, FF_ROW_TILE, D_MODEL), lambda i, j: (i, j, 0)),
            const(D_MODEL, 2 * D_FF), const(8, 2 * D_FF), const(D_FF, D_MODEL),
            const(1, D_MODEL), const(1, D_MODEL),
        ],
        out_specs=pl.BlockSpec((1, FF_ROW_TILE, D_MODEL), lambda i, j: (i, j, 0)),
        out_shape=jax.ShapeDtypeStruct((bsz, seq, D_MODEL), F32),
        scratch_shapes=[pltpu.VMEM((2 * D_FF // LANES, 8, LANES), F32),
                        pltpu.VMEM((2, 2 * FF_CHUNK // LANES, 8 + FF_ROW_TILE, LANES), F32),
                        pltpu.VMEM((2, FF_ROW_TILE, FF_CHUNK), BF16),
                        pltpu.VMEM((FF_ROW_TILE, D_MODEL), F32)],
        compiler_params=_params("arbitrary", "arbitrary"),
        name="ffn_ln",
    )(x3, w_up, taps, w_dn, g, b)


def _pad_lanes(v):
    return jnp.pad(v.astype(F32), (0, LANES - v.shape[0])).reshape(1, LANES)


def kernel(x, mem, w_in, b_f, lambda_q1, lambda_k1, lambda_q2, lambda_k2, diff_norm_g, w_o,
           ln1_g, ln1_b, w_mq, w_mkv, w_mo, ln2_g, ln2_b, w_up, conv_w, conv_b, w_down,
           ln3_g, ln3_b):
    bsz, seq, _ = x.shape
    depth = w_in.shape[0]
    alpha = (2 * depth) ** 0.25
    diff_slopes, dil_slopes = _alibi_slopes()
    slope_pairs = jnp.asarray(
        np.repeat(dil_slopes.reshape(DIL_HEADS // 2, 2), HEAD_DIM, axis=1)
        .reshape(DIL_HEADS // 2, 1, LANES))
    row2 = lambda v: v.astype(F32).reshape(1, D_MODEL)

    x2 = x.reshape(bsz * seq, D_MODEL)
    for l in range(depth):
        lam_init = 0.8 - 0.6 * math.exp(-0.3 * l)
        w_main, w_ff = _inproj_weights(w_in, l)
        h2, f2 = _inproj(x2, w_main, w_ff)
        h3 = h2.reshape(bsz, seq, MAIN_COLS)
        fox_qa, fox_ka = _fox_prep(f2.reshape(bsz, seq, LANES), _pad_lanes(b_f[l]))
        fox = _fox_attention(h3, fox_qa, fox_ka)
        lamp = jnp.zeros((8, LANES), F32)
        for i, v in enumerate((lambda_q1, lambda_k1, lambda_q2, lambda_k2)):
            lamp = lamp.at[i, :DIFF_QK_DIM].set(v[l].astype(F32))
        gain_row = jnp.tile(diff_norm_g[l].astype(F32), 2).reshape(1, LANES)
        diff = _diff_attention(h3, lamp, gain_row, diff_slopes, lam_init)
        dil = _dil_attention(h3, slope_pairs)
        kv = _memkv(mem, w_mkv[l].astype(BF16))
        x3 = _mix_mem_ln(x2.reshape(bsz, seq, D_MODEL), fox, diff, dil, w_o[l].astype(BF16),
                         row2(ln1_g[l]), row2(ln1_b[l]), kv, w_mq[l].astype(BF16),
                         w_mo[l].astype(BF16), row2(ln2_g[l]), row2(ln2_b[l]), alpha)
        taps = jnp.concatenate([conv_w[l][:, 0, :], conv_b[l][None, :],
                                jnp.zeros((4, 2 * D_FF), F32)], axis=0).astype(F32)
        x3 = _ffn_ln(x3, w_up[l].astype(BF16), taps, w_down[l].astype(BF16),
                     row2(ln3_g[l]), row2(ln3_b[l]), alpha)
        x2 = x3.reshape(bsz * seq, D_MODEL)
    return x2.reshape(bsz, seq, D_MODEL)
```
